```python
import jax, jax.numpy as jnp
from jax import lax
import numpy as np

D_MODEL = 2048
BATCH = 4
SEQ = 8192
DEPTH = 2
DEC_BATCH = 8
DEC_SEQ = 16
PAST_LEN = 2048

CHUNK = 64
QBLOCK = 128
D_MIX = D_MODEL
A_HEADS = 8
A_KV_HEADS = 2
A_HEAD_DIM = 128
A_GROUP = A_HEADS // A_KV_HEADS
A_WIDTH = A_HEADS * A_HEAD_DIM
IDX_HEADS = 8
IDX_DIM = 64
TOPK_MAX = 256
M_HEADS = 4
M_DIM = 128
M_WIDTH = M_HEADS * M_DIM
CONV_CH = 512
CONV_W = 31
D_FF = 4 * D_MODEL
ALPHA = (2 * DEPTH) ** 0.25
BETA = (8 * DEPTH) ** -0.25
LN_EPS = 1e-5
NEG_INF = -1e30

SPLIT_SIZES = (A_WIDTH, A_KV_HEADS * A_HEAD_DIM, A_KV_HEADS * A_HEAD_DIM,
               IDX_HEADS * IDX_DIM, IDX_DIM, IDX_HEADS,
               M_WIDTH, M_WIDTH, M_WIDTH, M_HEADS, M_HEADS, M_WIDTH,
               2 * CONV_CH)
IN_COLS = sum(SPLIT_SIZES)

kernel_name = "hybrid_stream_dsa_mlstm_conformer_step"


def _normalize(x):
    xf = x.astype(jnp.float32)
    mu = jnp.mean(xf, axis=-1, keepdims=True)
    var = jnp.mean(jnp.square(xf - mu), axis=-1, keepdims=True)
    return (xf - mu) * lax.rsqrt(var + LN_EPS)


def _layernorm(x, g, b):
    return (_normalize(x) * g + b).astype(x.dtype)


def _modulate(x, shift, scale):
    return (_normalize(x) * (1.0 + scale[:, None, :]) + shift[:, None, :]).astype(x.dtype)


def _split_proj(proj):
    parts, start = [], 0
    for size in SPLIT_SIZES:
        parts.append(proj[..., start:start + size])
        start += size
    return parts


def _dsa_attend(q, qi, wi, q_pos, k, v, ki, k_pos, topk):
    B, Tq = q.shape[:2]
    dots = jnp.einsum('bqhd,bsd->bqhs', qi, ki).astype(jnp.float32) * IDX_DIM ** -0.5
    score = jnp.einsum('bqh,bqhs->bqs', wi.astype(jnp.float32) * IDX_HEADS ** -0.5, jax.nn.relu(dots))
    visible = (k_pos[None, :] // CHUNK) <= (q_pos[:, None] // CHUNK)
    score = jnp.where(visible[None], score, NEG_INF)
    _, idx = lax.top_k(score, topk)
    valid = (k_pos[idx] // CHUNK) <= (q_pos[None, :, None] // CHUNK)
    k_sel = jax.vmap(lambda kb, ib: kb[ib])(k, idx)
    v_sel = jax.vmap(lambda vb, ib: vb[ib])(v, idx)
    qg = q.reshape(B, Tq, A_KV_HEADS, A_GROUP, A_HEAD_DIM)
    logits = jnp.einsum('bqngd,bqknd->bqngk', qg, k_sel).astype(jnp.float32) * A_HEAD_DIM ** -0.5
    logits = jnp.where(valid[:, :, None, None, :], logits, NEG_INF)
    p = jax.nn.softmax(logits, axis=-1).astype(v.dtype)
    out = jnp.einsum('bqngk,bqknd->bqngd', p, v_sel)
    return out.reshape(B, Tq, A_WIDTH)


def _dsa_prompt(q, qi, wi, k, v, ki, topk):
    B, T = q.shape[:2]
    nb = T // QBLOCK
    pos = jnp.arange(T, dtype=jnp.int32)

    def to_blocks(a):
        return jnp.moveaxis(a.reshape((B, nb, QBLOCK) + a.shape[2:]), 1, 0)

    def one_block(args):
        qb, qib, wib, pb = args
        return _dsa_attend(qb, qib, wib, pb, k, v, ki, pos, topk)

    out = lax.map(one_block, (to_blocks(q), to_blocks(qi), to_blocks(wi), pos.reshape(nb, QBLOCK)))
    return jnp.moveaxis(out, 0, 1).reshape(B, T, A_WIDTH)


def _mlstm_gates(mq, mk, mv, mi, mf, b_i, b_f):
    B, T = mq.shape[:2]
    q = mq.reshape(B, T, M_HEADS, M_DIM).astype(jnp.float32)
    k = mk.reshape(B, T, M_HEADS, M_DIM).astype(jnp.float32) * M_DIM ** -0.5
    v = mv.reshape(B, T, M_HEADS, M_DIM).astype(jnp.float32)
    ig = (mi + b_i).astype(jnp.float32)
    lf = jax.nn.log_sigmoid((mf + b_f).astype(jnp.float32))
    return (q, k, v, ig, lf)


def _mlstm_chunk(carry, inp):
    C, n, m = carry
    q, k, v, ig, lf = inp
    L = q.shape[1]
    b = jnp.cumsum(lf, axis=1)
    tri = jnp.tril(jnp.ones((L, L), dtype=bool))
    D = b[:, :, None, :] - b[:, None, :, :] + ig[:, None, :, :]
    D = jnp.where(tri[None, :, :, None], D, NEG_INF)
    inter = b + m[:, None, :]
    m_t = jnp.maximum(inter, jnp.max(D, axis=2))
    w_intra = jnp.exp(D - m_t[:, :, None, :])
    w_inter = jnp.exp(inter - m_t)
    s = jnp.einsum('bthd,bshd->btsh', q, k) * w_intra
    num = w_inter[..., None] * jnp.einsum('bthk,bhkv->bthv', q, C) + jnp.einsum('btsh,bshv->bthv', s, v)
    den = w_inter * jnp.einsum('bthk,bhk->bth', q, n) + jnp.sum(s, axis=2)
    h = num / jnp.maximum(jnp.abs(den), jnp.exp(-m_t))[..., None]
    m_end = m_t[:, -1]
    w_end = jnp.exp(b[:, -1:] - b + ig - m_end[:, None])
    decay = jnp.exp(b[:, -1] + m - m_end)
    C_new = decay[..., None, None] * C + jnp.einsum('bsh,bshk,bshv->bhkv', w_end, k, v)
    n_new = decay[..., None] * n + jnp.einsum('bsh,bshk->bhk', w_end, k)
    return (C_new, n_new, m_end), h


def _mlstm_prompt(inp):
    q = inp[0]
    B, T = q.shape[:2]
    nc = T // CHUNK

    def to_chunks(a):
        return jnp.moveaxis(a.reshape((B, nc, CHUNK) + a.shape[2:]), 1, 0)

    init = (jnp.zeros((B, M_HEADS, M_DIM, M_DIM), jnp.float32),
            jnp.zeros((B, M_HEADS, M_DIM), jnp.float32),
            jnp.zeros((B, M_HEADS), jnp.float32))
    state, h = lax.scan(_mlstm_chunk, init, tuple(to_chunks(a) for a in inp))
    return state, jnp.moveaxis(h, 0, 1).reshape(B, T, M_HEADS, M_DIM)


def _mlstm_out(h, o, g, dtype):
    B, T = o.shape[:2]
    h = jax.nn.sigmoid(o.astype(jnp.float32)).reshape(B, T, M_HEADS, M_DIM) * h
    return (_normalize(h).reshape(B, T, M_WIDTH) * g).astype(dtype)


def _conformer_conv(u, conv_state, w, bias, g, nb):
    a, gt = jnp.split(u, 2, axis=-1)
    glu = a * jax.nn.sigmoid(gt)
    full = jnp.concatenate([conv_state.astype(glu.dtype), glu], axis=1)
    y = lax.conv_general_dilated(full, w[:, None, :].astype(glu.dtype), window_strides=(1,),
                                 padding='VALID', dimension_numbers=('NWC', 'WIO', 'NWC'),
                                 feature_group_count=CONV_CH) + bias
    y = jax.nn.silu(_layernorm(y, g, nb))
    return y, full[:, -(CONV_W - 1):]


def _token_mixers(proj, p, cache):
    aq, ak, av, iq, ik, iw, mq, mk, mv, mi, mf, mo, cu = _split_proj(proj)
    B, T = proj.shape[:2]
    q = aq.reshape(B, T, A_HEADS, A_HEAD_DIM)
    k = ak.reshape(B, T, A_KV_HEADS, A_HEAD_DIM)
    v = av.reshape(B, T, A_KV_HEADS, A_HEAD_DIM)
    qi = iq.reshape(B, T, IDX_HEADS, IDX_DIM)
    m_in = _mlstm_gates(mq, mk, mv, mi, mf, p['b_igate'], p['b_fgate'])
    if cache is None:
        attn = _dsa_prompt(q, qi, iw, k, v, ik, min(TOPK_MAX, T // 4))
        (C, n, m), h = _mlstm_prompt(m_in)
        conv_state = jnp.zeros((B, CONV_W - 1, CONV_CH), proj.dtype)
    else:
        ck, cv, cik, cC, cn, cm, cconv = cache
        P = ck.shape[1]
        L = P + T
        attn = _dsa_attend(q, qi, iw, P + jnp.arange(T, dtype=jnp.int32),
                           jnp.concatenate([ck.astype(k.dtype), k], axis=1),
                           jnp.concatenate([cv.astype(v.dtype), v], axis=1),
                           jnp.concatenate([cik.astype(ik.dtype), ik], axis=1),
                           jnp.arange(L, dtype=jnp.int32), min(TOPK_MAX, L // 4))
        (C, n, m), h = _mlstm_chunk((cC.astype(jnp.float32), cn.astype(jnp.float32),
                                     cm.astype(jnp.float32)), m_in)
        conv_state = cconv
    mlstm = _mlstm_out(h, mo, p['mlstm_norm_g'], proj.dtype)
    conv, new_conv = _conformer_conv(cu, conv_state, p['conv_w'], p['conv_b'],
                                     p['conv_norm_g'], p['conv_norm_b'])
    mix = jnp.concatenate([attn, mlstm, conv], axis=-1)
    return mix, (k, v, ik, C, n, m, new_conv)


def _trunk_layer(x, c, p, cache):
    ada = jax.nn.silu(c) @ p['w_ada'] + p['b_ada']
    sh1, sc1, g1, sh2, sc2, g2 = jnp.split(ada, 6, axis=-1)
    proj = _modulate(x, sh1, sc1) @ p['w_in']
    mix, new_state = _token_mixers(proj, p, cache)
    y = mix @ p['w_out']
    x = _layernorm(ALPHA * x + (1.0 + g1[:, None, :]) * y, p['ln1_g'], p['ln1_b'])
    h = _modulate(x, sh2, sc2)
    f = jnp.square(jax.nn.relu(h @ p['w_ff1'])) @ p['w_ff2']
    x = _layernorm(ALPHA * x + (1.0 + g2[:, None, :]) * f, p['ln2_g'], p['ln2_b'])
    return x, new_state


def setup_inputs(seed: int = 0) -> dict:
    key = jax.random.key(seed)
    ks = jax.random.split(key, 32)

    def nrm(i, shape, s):
        return jax.random.normal(ks[i], shape, jnp.float32) * s

    return {
        'x_prompt': nrm(0, (BATCH, SEQ, D_MODEL), 1.0),
        'x_sample': nrm(1, (DEC_BATCH, DEC_SEQ, D_MODEL), 1.0),
        'c_prompt': nrm(2, (BATCH, D_MODEL), 1.0),
        'c_sample': nrm(3, (DEC_BATCH, D_MODEL), 1.0),
        'cache_attn_k': nrm(4, (DEPTH, DEC_BATCH, PAST_LEN, A_KV_HEADS, A_HEAD_DIM), 1.0),
        'cache_attn_v': nrm(5, (DEPTH, DEC_BATCH, PAST_LEN, A_KV_HEADS, A_HEAD_DIM), 1.0),
        'cache_idx_k': nrm(6, (DEPTH, DEC_BATCH, PAST_LEN, IDX_DIM), 1.0),
        'state_mlstm_C': nrm(7, (DEPTH, DEC_BATCH, M_HEADS, M_DIM, M_DIM), 0.1),
        'state_mlstm_n': nrm(8, (DEPTH, DEC_BATCH, M_HEADS, M_DIM), 0.1),
        'state_mlstm_m': nrm(9, (DEPTH, DEC_BATCH, M_HEADS), 1.0),
        'state_conv': nrm(10, (DEPTH, DEC_BATCH, CONV_W - 1, CONV_CH), 0.5),
        'w_ada': nrm(11, (DEPTH, D_MODEL, 6 * D_MODEL), 0.1 * D_MODEL ** -0.5),
        'b_ada': nrm(12, (DEPTH, 6 * D_MODEL), 0.02),
        'w_in': nrm(13, (DEPTH, D_MODEL, IN_COLS), D_MODEL ** -0.5),
        'b_igate': nrm(14, (DEPTH, M_HEADS), 0.5),
        'b_fgate': 3.0 + nrm(15, (DEPTH, M_HEADS), 0.5),
        'mlstm_norm_g': 1.0 + nrm(16, (DEPTH, M_WIDTH), 0.02),
        'conv_w': nrm(17, (DEPTH, CONV_W, CONV_CH), CONV_W ** -0.5),
        'conv_b': nrm(18, (DEPTH, CONV_CH), 0.02),
        'conv_norm_g': 1.0 + nrm(19, (DEPTH, CONV_CH), 0.02),
        'conv_norm_b': nrm(20, (DEPTH, CONV_CH), 0.02),
        'w_out': nrm(21, (DEPTH, D_MIX, D_MODEL), BETA * D_MIX ** -0.5),
        'ln1_g': 1.0 + nrm(22, (DEPTH, D_MODEL), 0.02),
        'ln1_b': nrm(23, (DEPTH, D_MODEL), 0.02),
        'w_ff1': nrm(24, (DEPTH, D_MODEL, D_FF), D_MODEL ** -0.5),
        'w_ff2': nrm(25, (DEPTH, D_FF, D_MODEL), BETA * D_FF ** -0.5),
        'ln2_g': 1.0 + nrm(26, (DEPTH, D_MODEL), 0.02),
        'ln2_b': nrm(27, (DEPTH, D_MODEL), 0.02),
    }


def reference(x_prompt, x_sample, c_prompt, c_sample, cache_attn_k, cache_attn_v, cache_idx_k,
              state_mlstm_C, state_mlstm_n, state_mlstm_m, state_conv,
              w_ada, b_ada, w_in, b_igate, b_fgate, mlstm_norm_g, conv_w, conv_b,
              conv_norm_g, conv_norm_b, w_out, ln1_g, ln1_b, w_ff1, w_ff2, ln2_g, ln2_b):
    xp, xs = x_prompt, x_sample
    st_p, st_s = [], []
    for l in range(DEPTH):
        p = {'w_ada': w_ada[l], 'b_ada': b_ada[l], 'w_in': w_in[l], 'b_igate': b_igate[l],
             'b_fgate': b_fgate[l], 'mlstm_norm_g': mlstm_norm_g[l], 'conv_w': conv_w[l],
             'conv_b': conv_b[l], 'conv_norm_g': conv_norm_g[l], 'conv_norm_b': conv_norm_b[l],
             'w_out': w_out[l], 'ln1_g': ln1_g[l], 'ln1_b': ln1_b[l], 'w_ff1': w_ff1[l],
             'w_ff2': w_ff2[l], 'ln2_g': ln2_g[l], 'ln2_b': ln2_b[l]}
        xp, sp = _trunk_layer(xp, c_prompt, p, None)
        cache = (cache_attn_k[l], cache_attn_v[l], cache_idx_k[l], state_mlstm_C[l],
                 state_mlstm_n[l], state_mlstm_m[l], state_conv[l])
        xs, ss = _trunk_layer(xs, c_sample, p, cache)
        st_p.append(sp)
        st_s.append(ss)

    def stack(sts, i):
        return jnp.stack([s[i] for s in sts], axis=0)

    return (xp, xs,
            stack(st_p, 0), stack(st_p, 1), stack(st_p, 2), stack(st_p, 3), stack(st_p, 4),
            stack(st_p, 5), stack(st_p, 6),
            stack(st_s, 0), stack(st_s, 1), stack(st_s, 2), stack(st_s, 3), stack(st_s, 4),
            stack(st_s, 5), stack(st_s, 6))
```

```python
import functools

import jax
import jax.numpy as jnp
from jax import lax
from jax.experimental import pallas as pl
from jax.experimental.pallas import tpu as pltpu

F32 = jnp.float32
BF16 = jnp.bfloat16

CHUNK = 64
A_HEADS = 8
A_KV_HEADS = 2
A_GROUP = A_HEADS // A_KV_HEADS
A_HEAD_DIM = 128
A_WIDTH = A_HEADS * A_HEAD_DIM
KV_WIDTH = A_KV_HEADS * A_HEAD_DIM
IDX_HEADS = 8
IDX_DIM = 64
TOPK_MAX = 256
M_HEADS = 4
M_DIM = 128
M_WIDTH = M_HEADS * M_DIM
CONV_CH = 512
CONV_W = 31
LN_EPS = 1e-5
NEG_INF = -1e30
INT_MIN = -2 ** 31

LANES = 128
VMEM_LIMIT_BYTES = 56 * 1024 * 1024

COL_Q = 0
COL_K = COL_Q + A_WIDTH
COL_V = COL_K + KV_WIDTH
COL_IQ = COL_V + KV_WIDTH
COL_MQ = COL_IQ + IDX_HEADS * IDX_DIM
COL_MK = COL_MQ + M_WIDTH
COL_MV = COL_MK + M_WIDTH
COL_MO = COL_MV + M_WIDTH
COL_CU = COL_MO + M_WIDTH
MAIN_COLS = COL_CU + 2 * CONV_CH
SM_IK = 0
SM_IW = SM_IK + IDX_DIM
SM_MI = SM_IW + IDX_HEADS
SM_MF = SM_MI + M_HEADS
SM_USED = SM_MF + M_HEADS

CONV_HALO = 32
CONV_PAD = CONV_HALO - (CONV_W - 1)


def _cparams(sem):
    return pltpu.CompilerParams(dimension_semantics=sem, vmem_limit_bytes=VMEM_LIMIT_BYTES)


def _normalize(x):
    mu = jnp.mean(x, axis=-1, keepdims=True)
    xc = x - mu
    var = jnp.mean(xc * xc, axis=-1, keepdims=True)
    return xc * lax.rsqrt(var + LN_EPS)


def _sigmoid(x):
    return 1.0 / (1.0 + jnp.exp(-x))


def _dot(a, b):
    return jnp.dot(a, b, preferred_element_type=F32)


def _dot_nt(a, b):
    return lax.dot_general(a, b, (((1,), (1,)), ((), ())), preferred_element_type=F32)


def _ada_kernel(c_ref, w_ref, b_ref, o_ref):
    c = c_ref[...]
    s = (c * _sigmoid(c)).astype(BF16)
    o_ref[0] = _dot(s, w_ref[0].astype(BF16)) + b_ref[0]


def _ada(c_all, w_ada, b_ada):
    depth, d, n = w_ada.shape
    rows = c_all.shape[0]
    tn = 1024
    return pl.pallas_call(
        _ada_kernel,
        grid=(depth, n // tn),
        in_specs=[pl.BlockSpec((rows, d), lambda l, j: (0, 0)),
                  pl.BlockSpec((1, d, tn), lambda l, j: (l, 0, j)),
                  pl.BlockSpec((1, 1, tn), lambda l, j: (l, 0, j))],
        out_specs=pl.BlockSpec((1, rows, tn), lambda l, j: (l, 0, j)),
        out_shape=jax.ShapeDtypeStruct((depth, rows, n), F32),
        compiler_params=_cparams(("arbitrary", "arbitrary")),
        name="ada",
    )(c_all, w_ada, b_ada.reshape(depth, 1, n))


def _mod_spec(per_row, tm, rows_per_batch, d):
    if per_row:
        return pl.BlockSpec((1, tm, d), lambda i, *_: (0, i, 0))
    bpb = rows_per_batch // tm
    return pl.BlockSpec((1, 1, d), lambda i, *_: (i // bpb, 0, 0))


def _inproj_kernel(x_ref, sh_ref, sc_ref, wm_ref, ws_ref, main_ref, kv_ref, small_ref, h_scr, *, kv_block):
    j = pl.program_id(1)

    @pl.when(j == 0)
    def _():
        h = _normalize(x_ref[...]) * (1.0 + sc_ref[0]) + sh_ref[0]
        hb = h.astype(BF16)
        h_scr[...] = hb
        small_ref[...] = _dot(hb, ws_ref[...])

    acc = _dot(h_scr[...], wm_ref[...])
    main_ref[...] = acc.astype(BF16)

    @pl.when(j == kv_block)
    def _():
        kv_ref[...] = acc


def _inproj(x, sh, sc, w_main, w_small, per_row, rows_per_batch, tm):
    m, d = x.shape
    tn = 2 * KV_WIDTH
    mod = _mod_spec(per_row, tm, rows_per_batch, d)
    return pl.pallas_call(
        functools.partial(_inproj_kernel, kv_block=COL_K // tn),
        grid=(m // tm, MAIN_COLS // tn),
        in_specs=[pl.BlockSpec((tm, d), lambda i, j: (i, 0)), mod, mod,
                  pl.BlockSpec((d, tn), lambda i, j: (0, j)),
                  pl.BlockSpec((d, LANES), lambda i, j: (0, 0))],
        out_specs=[pl.BlockSpec((tm, tn), lambda i, j: (i, j)),
                   pl.BlockSpec((tm, tn), lambda i, j: (i, 0)),
                   pl.BlockSpec((tm, LANES), lambda i, j: (i, 0))],
        out_shape=[jax.ShapeDtypeStruct((m, MAIN_COLS), BF16),
                   jax.ShapeDtypeStruct((m, tn), F32),
                   jax.ShapeDtypeStruct((m, LANES), F32)],
        scratch_shapes=[pltpu.VMEM((tm, d), BF16)],
        compiler_params=_cparams(("arbitrary", "arbitrary")),
        name="inproj",
    )(x, sh, sc, w_main, w_small)


def _outproj_kernel(a_ref, m_ref, c_ref, x_ref, g_ref, w_ref, lg_ref, lb_ref, o_ref, *, alpha):
    y = _dot(a_ref[...], w_ref[0:A_WIDTH, :])
    y += _dot(m_ref[...], w_ref[A_WIDTH:A_WIDTH + M_WIDTH, :])
    y += _dot(c_ref[...], w_ref[A_WIDTH + M_WIDTH:, :])
    z = alpha * x_ref[...] + (1.0 + g_ref[0]) * y
    o_ref[...] = _normalize(z) * lg_ref[...] + lb_ref[...]


def _outproj(attn, mls, cnv, x, gate, w_out, ln_g, ln_b, per_row, rows_per_batch, tm, alpha):
    m, d = x.shape
    row = lambda w: pl.BlockSpec((tm, w), lambda i: (i, 0))
    vec = pl.BlockSpec((1, d), lambda i: (0, 0))
    return pl.pallas_call(
        functools.partial(_outproj_kernel, alpha=alpha),
        grid=(m // tm,),
        in_specs=[row(A_WIDTH), row(M_WIDTH), row(CONV_CH), row(d),
                  _mod_spec(per_row, tm, rows_per_batch, d),
                  pl.BlockSpec(w_out.shape, lambda i: (0, 0)), vec, vec],
        out_specs=row(d),
        out_shape=jax.ShapeDtypeStruct((m, d), F32),
        compiler_params=_cparams(("arbitrary",)),
        name="outproj",
    )(attn, mls, cnv, x, gate, w_out, ln_g.reshape(1, d), ln_b.reshape(1, d))


def _ffn_kernel(x_ref, sh_ref, sc_ref, g_ref, w1_ref, w2_ref, lg_ref, lb_ref, o_ref, h_scr, acc_scr, *, alpha):
    j = pl.program_id(1)

    @pl.when(j == 0)
    def _():
        h = _normalize(x_ref[...]) * (1.0 + sc_ref[0]) + sh_ref[0]
        h_scr[...] = h.astype(BF16)
        acc_scr[...] = jnp.zeros_like(acc_scr)

    u = jnp.maximum(_dot(h_scr[...], w1_ref[...]), 0.0)
    acc_scr[...] += _dot((u * u).astype(BF16), w2_ref[...])

    @pl.when(j == pl.num_programs(1) - 1)
    def _():
        z = alpha * x_ref[...] + (1.0 + g_ref[0]) * acc_scr[...]
        o_ref[...] = _normalize(z) * lg_ref[...] + lb_ref[...]


def _ffn(x, sh, sc, gate, w1, w2, ln_g, ln_b, per_row, rows_per_batch, tm, alpha):
    m, d = x.shape
    dff = w1.shape[1]
    tf = 512
    mod = _mod_spec(per_row, tm, rows_per_batch, d)
    vec = pl.BlockSpec((1, d), lambda i, j: (0, 0))
    return pl.pallas_call(
        functools.partial(_ffn_kernel, alpha=alpha),
        grid=(m // tm, dff // tf),
        in_specs=[pl.BlockSpec((tm, d), lambda i, j: (i, 0)), mod, mod, mod,
                  pl.BlockSpec((d, tf), lambda i, j: (0, j)),
                  pl.BlockSpec((tf, d), lambda i, j: (j, 0)), vec, vec],
        out_specs=pl.BlockSpec((tm, d), lambda i, j: (i, 0)),
        out_shape=jax.ShapeDtypeStruct((m, d), F32),
        scratch_shapes=[pltpu.VMEM((tm, d), BF16), pltpu.VMEM((tm, d), F32)],
        compiler_params=_cparams(("arbitrary", "arbitrary")),
        name="ffn",
    )(x, sh, sc, gate, w1, w2, ln_g.reshape(1, d), ln_b.reshape(1, d))


def _dsa_kernel(q_ref, qi_ref, wi_ref, k_ref, v_ref, ki_ref, o_ref, key_scr, m_scr, l_scr, acc_scr,
                *, tq, ts, q_off, n_keys, topk, n_tiles_static):
    qb = pl.program_id(1)
    q_pos0 = q_off + qb * tq
    if n_tiles_static is None:
        vis_max = jnp.minimum(((q_pos0 + tq - 1) // CHUNK + 1) * CHUNK, n_keys)
        n_tiles = (vis_max + ts - 1) // ts
    else:
        n_tiles = n_tiles_static

    row_chunk = (q_pos0 + lax.broadcasted_iota(jnp.int32, (tq, 1), 0)) // CHUNK
    n_vis = jnp.minimum((row_chunk + 1) * CHUNK, n_keys)
    kk = jnp.minimum(topk, n_vis).astype(F32)

    qi = qi_ref[0]
    wi = wi_ref[0] * IDX_HEADS ** -0.5

    def score_tile(t, carry):
        start = pl.multiple_of(t * ts, ts)
        kt = ki_ref[0, pl.ds(start, ts), :]
        score = jnp.zeros((tq, ts), F32)
        for h in range(IDX_HEADS):
            d = _dot_nt(qi[:, h * IDX_DIM:(h + 1) * IDX_DIM], kt) * IDX_DIM ** -0.5
            score = score + wi[:, h:h + 1] * jnp.maximum(d, 0.0)
        score = jnp.where(score == 0.0, 0.0, score)
        bits = lax.bitcast_convert_type(score, jnp.int32)
        key = bits ^ ((bits >> 31) & 0x7FFFFFFF)
        col = start + lax.broadcasted_iota(jnp.int32, (1, ts), 1)
        visible = (col // CHUNK <= row_chunk) & (col < n_keys)
        key_scr[t] = jnp.where(visible, key, INT_MIN)
        return carry

    lax.fori_loop(0, n_tiles, score_tile, 0)

    def count(pred):
        def body(t, acc):
            hit = jnp.where(pred(key_scr[t]), 1.0, 0.0)
            for c in range(ts // LANES):
                acc = acc + hit[:, c * LANES:(c + 1) * LANES]
            return acc
        acc = lax.fori_loop(0, n_tiles, body, jnp.zeros((tq, LANES), F32))
        return jnp.sum(acc, axis=1, keepdims=True)

    def bisect(i, t_u):
        trial = t_u | lax.shift_left(jnp.int32(1), 31 - i)
        cand = trial ^ INT_MIN
        c = count(lambda blk: blk >= cand)
        return jnp.where(c >= kk, trial, t_u)

    t_u = lax.fori_loop(0, 32, bisect, jnp.zeros((tq, 1), jnp.int32))
    thr = t_u ^ INT_MIN

    n_ge = count(lambda blk: blk >= thr)

    @pl.when(jnp.max(n_ge - kk) > 0.0)
    def _():
        n_gt = count(lambda blk: blk > thr)
        keep = kk - n_gt
        upper = (lax.broadcasted_iota(jnp.int32, (ts, ts), 0)
                 <= lax.broadcasted_iota(jnp.int32, (ts, ts), 1)).astype(BF16)

        def tie_tile(t, seen):
            blk = key_scr[t]
            eq = blk == thr
            rank = seen + _dot(jnp.where(eq, 1.0, 0.0).astype(BF16), upper)
            key_scr[t] = jnp.where(eq & (rank > keep), INT_MIN, blk)
            return rank[:, ts - 1:ts]

        lax.fori_loop(0, n_tiles, tie_tile, jnp.zeros((tq, 1), F32))

    m_scr[...] = jnp.full(m_scr.shape, NEG_INF, F32)
    l_scr[...] = jnp.zeros(l_scr.shape, F32)
    acc_scr[...] = jnp.zeros(acc_scr.shape, F32)

    def attend_tile(t, carry):
        start = pl.multiple_of(t * ts, ts)
        mask = key_scr[t] >= thr
        for n in range(A_KV_HEADS):
            kt = k_ref[0, pl.ds(start, ts), n * A_HEAD_DIM:(n + 1) * A_HEAD_DIM]
            vt = v_ref[0, pl.ds(start, ts), n * A_HEAD_DIM:(n + 1) * A_HEAD_DIM]
            for g in range(A_GROUP):
                h = n * A_GROUP + g
                qh = q_ref[0, :, h * A_HEAD_DIM:(h + 1) * A_HEAD_DIM]
                s = _dot_nt(qh, kt) * A_HEAD_DIM ** -0.5
                s = jnp.where(mask, s, NEG_INF)
                m_old = m_scr[h]
                m_new = jnp.maximum(m_old, jnp.max(s, axis=1, keepdims=True))
                p = jnp.where(mask, jnp.exp(s - m_new), 0.0)
                alpha = jnp.exp(m_old - m_new)
                l_scr[h] = alpha * l_scr[h] + jnp.sum(p, axis=1, keepdims=True)
                acc_scr[h] = alpha * acc_scr[h] + _dot(p.astype(BF16), vt)
                m_scr[h] = m_new
        return carry

    lax.fori_loop(0, n_tiles, attend_tile, 0)

    for h in range(A_HEADS):
        o_ref[0, :, h * A_HEAD_DIM:(h + 1) * A_HEAD_DIM] = (acc_scr[h] / l_scr[h]).astype(BF16)


def _dsa(main3, keys_src, ki, wi, *, k_col, v_col, q_off, n_keys, tq, ts):
    b, t_q, _ = main3.shape
    s_len = keys_src.shape[1]
    topk = min(TOPK_MAX, n_keys // 4)
    dynamic = q_off == 0
    kern = functools.partial(_dsa_kernel, tq=tq, ts=ts, q_off=q_off, n_keys=n_keys, topk=topk,
                             n_tiles_static=None if dynamic else s_len // ts)
    return pl.pallas_call(
        kern,
        grid=(b, t_q // tq),
        in_specs=[pl.BlockSpec((1, tq, A_WIDTH), lambda i, j: (i, j, COL_Q // A_WIDTH)),
                  pl.BlockSpec((1, tq, IDX_HEADS * IDX_DIM), lambda i, j: (i, j, COL_IQ // (IDX_HEADS * IDX_DIM))),
                  pl.BlockSpec((1, tq, IDX_HEADS), lambda i, j: (i, j, 0)),
                  pl.BlockSpec((1, s_len, KV_WIDTH), lambda i, j: (i, 0, k_col)),
                  pl.BlockSpec((1, s_len, KV_WIDTH), lambda i, j: (i, 0, v_col)),
                  pl.BlockSpec((1, s_len, IDX_DIM), lambda i, j: (i, 0, 0))],
        out_specs=pl.BlockSpec((1, tq, A_WIDTH), lambda i, j: (i, j, 0)),
        out_shape=jax.ShapeDtypeStruct((b, t_q, A_WIDTH), BF16),
        scratch_shapes=[pltpu.VMEM((s_len // ts, tq, ts), jnp.int32),
                        pltpu.VMEM((A_HEADS, tq, 1), F32),
                        pltpu.VMEM((A_HEADS, tq, 1), F32),
                        pltpu.VMEM((A_HEADS, tq, A_HEAD_DIM), F32)],
        compiler_params=_cparams(("arbitrary", "arbitrary")),
        name="dsa",
    )(main3, main3, wi, keys_src, keys_src, ki)


def _log_sigmoid(x):
    return jnp.minimum(x, 0.0) - jnp.log1p(jnp.exp(-jnp.abs(x)))


def _mlstm_kernel(q_ref, k_ref, v_ref, o_ref, g_ref, gb_ref, ng_ref, c0_ref, n0_ref, m0_ref,
                  y_ref, c_out, n_out, m_out, c_scr, n_scr, m_scr, *, chunk, valid_len):
    step = pl.program_id(1)

    @pl.when(step == 0)
    def _():
        c_scr[...] = c0_ref[0]
        n_scr[...] = n0_ref[0]
        m_scr[...] = m0_ref[0]

    L = chunk
    g = g_ref[0] + gb_ref[...]
    lf = _log_sigmoid(g)
    if valid_len < L:
        live = lax.broadcasted_iota(jnp.int32, (L, 1), 0) < valid_len
        lf = jnp.where(live, lf, 0.0)
        g = jnp.where(live, g, NEG_INF)
    g_t = g.T
    lf_t = lf.T
    row = lax.broadcasted_iota(jnp.int32, (L, L), 0)
    col = lax.broadcasted_iota(jnp.int32, (L, L), 1)
    lower = row >= col
    b_c = jnp.dot(jnp.where(lower, 1.0, 0.0), lf, precision=lax.Precision.HIGHEST,
                  preferred_element_type=F32)
    b_r = jnp.dot(lf_t, jnp.where(lower, 0.0, 1.0) + jnp.where(row == col, 1.0, 0.0),
                  precision=lax.Precision.HIGHEST, preferred_element_type=F32)

    for h in range(M_HEADS):
        sl = slice(h * M_DIM, (h + 1) * M_DIM)
        b_col = b_c[:, SM_MF + h:SM_MF + h + 1]
        ig_col = g[:, SM_MI + h:SM_MI + h + 1]
        a_row = g_t[SM_MI + h:SM_MI + h + 1, :] - b_r[SM_MF + h:SM_MF + h + 1, :]
        m_prev = m_scr[h, 0:1, 0:1]
        d_mat = jnp.where(lower, b_col + a_row, NEG_INF)
        inter = b_col + m_prev
        m_t = jnp.maximum(inter, jnp.max(d_mat, axis=1, keepdims=True))
        w_intra = jnp.exp(d_mat - m_t)
        w_inter = jnp.exp(inter - m_t)
        qh = q_ref[0, :, sl]
        kf = k_ref[0, :, sl].astype(F32) * M_DIM ** -0.5
        vh = v_ref[0, :, sl]
        s = _dot_nt(qh, kf.astype(BF16)) * w_intra
        c_old = c_scr[h]
        n_old = n_scr[h, 0:1, :]
        num = w_inter * _dot(qh, c_old.astype(BF16)) + _dot(s.astype(BF16), vh)
        den = (w_inter * jnp.sum(qh.astype(F32) * n_old, axis=1, keepdims=True)
               + jnp.sum(s, axis=1, keepdims=True))
        hid = num / jnp.maximum(jnp.abs(den), jnp.exp(-m_t))
        m_end = m_t[L - 1:L, :]
        b_last = b_col[L - 1:L, :]
        w_end = jnp.exp(b_last - b_col + ig_col - m_end)
        decay = jnp.exp(b_last + m_prev - m_end)
        kw = w_end * kf
        c_scr[h] = decay * c_old + _dot(kw.T.astype(BF16), vh)
        n_scr[h] = jnp.broadcast_to(decay * n_old + jnp.sum(kw, axis=0, keepdims=True), (8, M_DIM))
        m_scr[h] = jnp.broadcast_to(m_end, (8, LANES))
        gated = _sigmoid(o_ref[0, :, sl].astype(F32)) * hid
        y_ref[0, :, sl] = (_normalize(gated) * ng_ref[:, sl]).astype(BF16)

    c_out[0] = c_scr[...]
    n_out[0] = n_scr[...]
    m_out[0] = m_scr[...]


def _mlstm(main3, small3, gate_bias, norm_g, c0, n0, m0, *, chunk, valid_len):
    b, t, _ = main3.shape
    blk = lambda col: pl.BlockSpec((1, chunk, M_WIDTH), lambda i, j: (i, j, col // M_WIDTH))
    st = lambda shape: pl.BlockSpec((1,) + shape, lambda i, j: (i,) + (0,) * len(shape))
    c_shape, n_shape, m_shape = (M_HEADS, M_DIM, M_DIM), (M_HEADS, 8, M_DIM), (M_HEADS, 8, LANES)
    return pl.pallas_call(
        functools.partial(_mlstm_kernel, chunk=chunk, valid_len=valid_len),
        grid=(b, t // chunk),
        in_specs=[blk(COL_MQ), blk(COL_MK), blk(COL_MV), blk(COL_MO),
                  pl.BlockSpec((1, chunk, LANES), lambda i, j: (i, j, 0)),
                  pl.BlockSpec((1, LANES), lambda i, j: (0, 0)),
                  pl.BlockSpec((1, M_WIDTH), lambda i, j: (0, 0)),
                  st(c_shape), st(n_shape), st(m_shape)],
        out_specs=[pl.BlockSpec((1, chunk, M_WIDTH), lambda i, j: (i, j, 0)),
                   st(c_shape), st(n_shape), st(m_shape)],
        out_shape=[jax.ShapeDtypeStruct((b, t, M_WIDTH), BF16),
                   jax.ShapeDtypeStruct((b,) + c_shape, F32),
                   jax.ShapeDtypeStruct((b,) + n_shape, F32),
                   jax.ShapeDtypeStruct((b,) + m_shape, F32)],
        scratch_shapes=[pltpu.VMEM(c_shape, F32), pltpu.VMEM(n_shape, F32), pltpu.VMEM(m_shape, F32)],
        compiler_params=_cparams(("arbitrary", "arbitrary")),
        name="mlstm",
    )(main3, main3, main3, main3, small3, gate_bias, norm_g.reshape(1, M_WIDTH), c0, n0, m0)


def _conv_kernel(cu_ref, halo_ref, st_ref, w_ref, cb_ref, ng_ref, nb_ref, y_ref, ns_ref, full_scr,
                 *, tm, valid_len):
    i = pl.program_id(1)

    def glu(u):
        return u[:, :CONV_CH].astype(F32) * _sigmoid(u[:, CONV_CH:].astype(F32))

    @pl.when(i == 0)
    def _():
        full_scr[0:CONV_HALO, :] = st_ref[0]

    @pl.when(i > 0)
    def _():
        full_scr[0:CONV_HALO, :] = glu(halo_ref[0])

    full_scr[CONV_HALO:CONV_HALO + tm, :] = glu(cu_ref[0])
    acc = jnp.zeros((tm, CONV_CH), F32)
    for j in range(CONV_W):
        acc = acc + w_ref[j:j + 1, :] * full_scr[pl.ds(CONV_PAD + j, tm), :]
    y = _normalize(acc + cb_ref[...]) * ng_ref[...] + nb_ref[...]
    y_ref[0] = (y * _sigmoid(y)).astype(BF16)
    ns_ref[0] = full_scr[pl.ds(valid_len, CONV_HALO), :]


def _conv(main3, state, w, bias, ng, nb, *, tm, valid_len):
    b, t, _ = main3.shape
    cu_blk = COL_CU // (2 * CONV_CH)
    hpb = tm // CONV_HALO
    vec = pl.BlockSpec((1, CONV_CH), lambda i, j: (0, 0))
    return pl.pallas_call(
        functools.partial(_conv_kernel, tm=tm, valid_len=valid_len),
        grid=(b, t // tm),
        in_specs=[pl.BlockSpec((1, tm, 2 * CONV_CH), lambda i, j: (i, j, cu_blk)),
                  pl.BlockSpec((1, CONV_HALO, 2 * CONV_CH), lambda i, j: (i, jnp.maximum(j * hpb - 1, 0), cu_blk)),
                  pl.BlockSpec((1, CONV_HALO, CONV_CH), lambda i, j: (i, 0, 0)),
                  pl.BlockSpec((CONV_W, CONV_CH), lambda i, j: (0, 0)), vec, vec, vec],
        out_specs=[pl.BlockSpec((1, tm, CONV_CH), lambda i, j: (i, j, 0)),
                   pl.BlockSpec((1, CONV_HALO, CONV_CH), lambda i, j: (i, 0, 0))],
        out_shape=[jax.ShapeDtypeStruct((b, t, CONV_CH), BF16),
                   jax.ShapeDtypeStruct((b, CONV_HALO, CONV_CH), F32)],
        scratch_shapes=[pltpu.VMEM((CONV_HALO + tm, CONV_CH), F32)],
        compiler_params=_cparams(("arbitrary", "arbitrary")),
        name="conv",
    )(main3, main3, state, w, bias.reshape(1, CONV_CH), ng.reshape(1, CONV_CH), nb.reshape(1, CONV_CH))


def _split_w_in(w_in):
    sizes = (A_WIDTH, KV_WIDTH, KV_WIDTH, IDX_HEADS * IDX_DIM, IDX_DIM, IDX_HEADS,
             M_WIDTH, M_WIDTH, M_WIDTH, M_HEADS, M_HEADS, M_WIDTH, 2 * CONV_CH)
    parts, start = [], 0
    for size in sizes:
        parts.append(w_in[:, start:start + size])
        start += size
    aq, ak, av, iq, ik, iw, mq, mk, mv, mi, mf, mo, cu = parts
    w_main = jnp.concatenate([aq, ak, av, iq, mq, mk, mv, mo, cu], axis=1).astype(BF16)
    pad = jnp.zeros((w_in.shape[0], LANES - SM_USED), w_in.dtype)
    w_small = jnp.concatenate([ik, iw, mi, mf, pad], axis=1).astype(BF16)
    return w_main, w_small


def _layer(x, ada, p, cache, *, rows_per_batch, tm, alpha):
    m, d = x.shape
    nb = m // rows_per_batch
    t = rows_per_batch
    per_row = cache is not None
    if per_row:
        mods = [jnp.repeat(v, t, axis=0)[None] for v in jnp.split(ada, 6, axis=-1)]
    else:
        mods = [v[:, None, :] for v in jnp.split(ada, 6, axis=-1)]
    sh1, sc1, g1, sh2, sc2, g2 = mods

    main, kv, small = _inproj(x, sh1, sc1, p['w_main'], p['w_small'], per_row, t, tm)
    k_new = kv[:, :KV_WIDTH].reshape(nb, t, A_KV_HEADS, A_HEAD_DIM)
    v_new = kv[:, KV_WIDTH:].reshape(nb, t, A_KV_HEADS, A_HEAD_DIM)
    ik_new = small[:, SM_IK:SM_IK + IDX_DIM].reshape(nb, t, IDX_DIM)

    gate_bias = jnp.zeros((1, LANES), F32)
    gate_bias = gate_bias.at[0, SM_MI:SM_MI + M_HEADS].set(p['b_igate'])
    gate_bias = gate_bias.at[0, SM_MF:SM_MF + M_HEADS].set(p['b_fgate'])

    if cache is None:
        main3 = main.reshape(nb, t, MAIN_COLS)
        small3 = small.reshape(nb, t, LANES)
        ki = ik_new.astype(BF16)
        wi = small3[:, :, SM_IW:SM_IW + IDX_HEADS]
        attn = _dsa(main3, main3, ki, wi, k_col=COL_K // KV_WIDTH, v_col=COL_V // KV_WIDTH,
                    q_off=0, n_keys=t, tq=min(128, t), ts=min(512, t))
        c0 = jnp.zeros((nb, M_HEADS, M_DIM, M_DIM), F32)
        n0 = jnp.zeros((nb, M_HEADS, 8, M_DIM), F32)
        m0 = jnp.zeros((nb, M_HEADS, 8, LANES), F32)
        mls, c_new, n_new, m_new = _mlstm(main3, small3, gate_bias, p['mlstm_norm_g'], c0, n0, m0,
                                          chunk=min(256, t), valid_len=min(256, t))
        state = jnp.zeros((nb, CONV_HALO, CONV_CH), F32)
        tc = min(512, t)
        cnv, conv_new = _conv(main3, state, p['conv_w'], p['conv_b'], p['conv_norm_g'], p['conv_norm_b'],
                              tm=tc, valid_len=tc)
        attn, mls, cnv = (a.reshape(m, -1) for a in (attn, mls, cnv))
    else:
        ck, cv, cik, c_st, n_st, m_st, conv_st = cache
        past = ck.shape[1]
        n_keys = past + t
        tp = 128
        ts = 512
        s_pad = -(-n_keys // ts) * ts
        main3 = jnp.pad(main.reshape(nb, t, MAIN_COLS), ((0, 0), (0, tp - t), (0, 0)))
        small3 = jnp.pad(small.reshape(nb, t, LANES), ((0, 0), (0, tp - t), (0, 0)))
        pad_keys = lambda a: jnp.pad(a, ((0, 0), (0, s_pad - n_keys), (0, 0)))
        k_all = pad_keys(jnp.concatenate([ck.reshape(nb, past, KV_WIDTH), kv[:, :KV_WIDTH].reshape(nb, t, KV_WIDTH)], axis=1))
        v_all = pad_keys(jnp.concatenate([cv.reshape(nb, past, KV_WIDTH), kv[:, KV_WIDTH:].reshape(nb, t, KV_WIDTH)], axis=1))
        kv_all = jnp.concatenate([k_all, v_all], axis=-1).astype(BF16)
        ki = pad_keys(jnp.concatenate([cik, ik_new], axis=1)).astype(BF16)
        wi = small3[:, :, SM_IW:SM_IW + IDX_HEADS]
        attn = _dsa(main3, kv_all, ki, wi, k_col=0, v_col=1, q_off=past, n_keys=n_keys, tq=tp, ts=ts)
        n0 = jnp.broadcast_to(n_st[:, :, None, :], (nb, M_HEADS, 8, M_DIM))
        m0 = jnp.broadcast_to(m_st[:, :, None, None], (nb, M_HEADS, 8, LANES))
        mls, c_new, n_new, m_new = _mlstm(main3, small3, gate_bias, p['mlstm_norm_g'], c_st, n0, m0,
                                          chunk=tp, valid_len=t)
        state = jnp.pad(conv_st, ((0, 0), (CONV_PAD, 0), (0, 0)))
        cnv, conv_new = _conv(main3, state, p['conv_w'], p['conv_b'], p['conv_norm_g'], p['conv_norm_b'],
                              tm=tp, valid_len=t)
        attn, mls, cnv = (a[:, :t].reshape(m, -1) for a in (attn, mls, cnv))

    x = _outproj(attn, mls, cnv, x, g1, p['w_out'], p['ln1_g'], p['ln1_b'], per_row, t, tm, alpha)
    x = _ffn(x, sh2, sc2, g2, p['w_ff1'], p['w_ff2'], p['ln2_g'], p['ln2_b'], per_row, t, tm, alpha)
    new_state = (k_new, v_new, ik_new, c_new, n_new[:, :, 0, :], m_new[:, :, 0, 0], conv_new[:, CONV_PAD:, :])
    return x, new_state


def kernel(x_prompt, x_sample, c_prompt, c_sample, cache_attn_k, cache_attn_v, cache_idx_k,
           state_mlstm_C, state_mlstm_n, state_mlstm_m, state_conv,
           w_ada, b_ada, w_in, b_igate, b_fgate, mlstm_norm_g, conv_w, conv_b,
           conv_norm_g, conv_norm_b, w_out, ln1_g, ln1_b, w_ff1, w_ff2, ln2_g, ln2_b):
    depth = w_ada.shape[0]
    alpha = (2 * depth) ** 0.25
    bp, tp, d = x_prompt.shape
    bs, tsmp, _ = x_sample.shape

    c_all = jnp.concatenate([c_prompt, c_sample], axis=0)
    rows = -(-c_all.shape[0] // 8) * 8
    ada_all = _ada(jnp.pad(c_all, ((0, rows - c_all.shape[0]), (0, 0))), w_ada, b_ada)

    xp = x_prompt.reshape(bp * tp, d)
    xs = x_sample.reshape(bs * tsmp, d)
    st_p, st_s = [], []
    for l in range(depth):
        w_main, w_small = _split_w_in(w_in[l])
        p = {'w_main': w_main, 'w_small': w_small, 'b_igate': b_igate[l], 'b_fgate': b_fgate[l],
             'mlstm_norm_g': mlstm_norm_g[l], 'conv_w': conv_w[l], 'conv_b': conv_b[l],
             'conv_norm_g': conv_norm_g[l], 'conv_norm_b': conv_norm_b[l],
             'w_out': w_out[l].astype(BF16), 'ln1_g': ln1_g[l], 'ln1_b': ln1_b[l],
             'w_ff1': w_ff1[l].astype(BF16), 'w_ff2': w_ff2[l].astype(BF16),
             'ln2_g': ln2_g[l], 'ln2_b': ln2_b[l]}
        xp, sp = _layer(xp, ada_all[l, :bp], p, None, rows_per_batch=tp, tm=min(512, tp), alpha=alpha)
        cache = (cache_attn_k[l], cache_attn_v[l], cache_idx_k[l], state_mlstm_C[l],
                 state_mlstm_n[l], state_mlstm_m[l], state_conv[l])
        xs, ss = _layer(xs, ada_all[l, bp:bp + bs], p, cache, rows_per_batch=tsmp, tm=bs * tsmp, alpha=alpha)
        st_p.append(sp)
        st_s.append(ss)

    def stack(sts, i):
        return jnp.stack([s[i] for s in sts], axis=0)

    return (xp.reshape(bp, tp, d), xs.reshape(bs, tsmp, d),
            *(stack(st_p, i) for i in range(7)), *(stack(st_s, i) for i in range(7)))
```

```python
import functools

import jax
import jax.numpy as jnp
from jax import lax
from jax.experimental import pallas as pl
from jax.experimental.pallas import tpu as pltpu

F32 = jnp.float32
BF16 = jnp.bfloat16

CHUNK = 64
A_HEADS = 8
A_KV_HEADS = 2
A_GROUP = A_HEADS // A_KV_HEADS
A_HEAD_DIM = 128
A_WIDTH = A_HEADS * A_HEAD_DIM
KV_WIDTH = A_KV_HEADS * A_HEAD_DIM
IDX_HEADS = 8
IDX_DIM = 64
TOPK_MAX = 256
M_HEADS = 4
M_DIM = 128
M_WIDTH = M_HEADS * M_DIM
CONV_CH = 512
CONV_W = 31
LN_EPS = 1e-5
NEG_INF = -1e30
ATTN_LOG2E_SCALE = A_HEAD_DIM ** -0.5 * 1.4426950408889634
INT_MIN = -2 ** 31

LANES = 128
VMEM_LIMIT_BYTES = 56 * 1024 * 1024

COL_Q = 0
COL_K = COL_Q + A_WIDTH
COL_V = COL_K + KV_WIDTH
COL_IQ = COL_V + KV_WIDTH
COL_MQ = COL_IQ + IDX_HEADS * IDX_DIM
COL_MK = COL_MQ + M_WIDTH
COL_MV = COL_MK + M_WIDTH
COL_MO = COL_MV + M_WIDTH
COL_CU = COL_MO + M_WIDTH
MAIN_COLS = COL_CU + 2 * CONV_CH
SM_IK = 0
SM_IW = SM_IK + IDX_DIM
SM_MI = SM_IW + IDX_HEADS
SM_MF = SM_MI + M_HEADS
SM_USED = SM_MF + M_HEADS

CONV_HALO = 32
CONV_PAD = CONV_HALO - (CONV_W - 1)


def _cparams(sem):
    return pltpu.CompilerParams(dimension_semantics=sem, vmem_limit_bytes=VMEM_LIMIT_BYTES)


def _normalize(x):
    mu = jnp.mean(x, axis=-1, keepdims=True)
    xc = x - mu
    var = jnp.mean(xc * xc, axis=-1, keepdims=True)
    return xc * lax.rsqrt(var + LN_EPS)


def _sigmoid(x):
    return 1.0 / (1.0 + jnp.exp(-x))


def _dot(a, b):
    return jnp.dot(a, b, preferred_element_type=F32)


def _dot_nt(a, b):
    return lax.dot_general(a, b, (((1,), (1,)), ((), ())), preferred_element_type=F32)


def _ada_kernel(c_ref, w_ref, b_ref, o_ref):
    c = c_ref[...]
    s = (c * _sigmoid(c)).astype(BF16)
    o_ref[0] = _dot(s, w_ref[0].astype(BF16)) + b_ref[0]


def _ada(c_all, w_ada, b_ada):
    depth, d, n = w_ada.shape
    rows = c_all.shape[0]
    tn = 1024
    return pl.pallas_call(
        _ada_kernel,
        grid=(depth, n // tn),
        in_specs=[pl.BlockSpec((rows, d), lambda l, j: (0, 0)),
                  pl.BlockSpec((1, d, tn), lambda l, j: (l, 0, j)),
                  pl.BlockSpec((1, 1, tn), lambda l, j: (l, 0, j))],
        out_specs=pl.BlockSpec((1, rows, tn), lambda l, j: (l, 0, j)),
        out_shape=jax.ShapeDtypeStruct((depth, rows, n), F32),
        compiler_params=_cparams(("arbitrary", "arbitrary")),
        name="ada",
    )(c_all, w_ada, b_ada.reshape(depth, 1, n))


def _mod_spec(per_row, tm, rows_per_batch, d):
    if per_row:
        return pl.BlockSpec((1, tm, d), lambda i, *_: (0, i, 0))
    bpb = rows_per_batch // tm
    return pl.BlockSpec((1, 1, d), lambda i, *_: (i // bpb, 0, 0))


def _inproj_kernel(x_ref, sh_ref, sc_ref, wm_ref, ws_ref, main_ref, kv_ref, small_ref, h_scr, *, kv_block):
    j = pl.program_id(1)

    @pl.when(j == 0)
    def _():
        h = _normalize(x_ref[...]) * (1.0 + sc_ref[0]) + sh_ref[0]
        hb = h.astype(BF16)
        h_scr[...] = hb
        small_ref[...] = _dot(hb, ws_ref[...])

    acc = _dot(h_scr[...], wm_ref[...])
    main_ref[...] = acc.astype(BF16)

    @pl.when(j == kv_block)
    def _():
        kv_ref[...] = acc


def _inproj(x, sh, sc, w_main, w_small, per_row, rows_per_batch, tm):
    m, d = x.shape
    tn = 2 * KV_WIDTH
    mod = _mod_spec(per_row, tm, rows_per_batch, d)
    return pl.pallas_call(
        functools.partial(_inproj_kernel, kv_block=COL_K // tn),
        grid=(m // tm, MAIN_COLS // tn),
        in_specs=[pl.BlockSpec((tm, d), lambda i, j: (i, 0)), mod, mod,
                  pl.BlockSpec((d, tn), lambda i, j: (0, j)),
                  pl.BlockSpec((d, LANES), lambda i, j: (0, 0))],
        out_specs=[pl.BlockSpec((tm, tn), lambda i, j: (i, j)),
                   pl.BlockSpec((tm, tn), lambda i, j: (i, 0)),
                   pl.BlockSpec((tm, LANES), lambda i, j: (i, 0))],
        out_shape=[jax.ShapeDtypeStruct((m, MAIN_COLS), BF16),
                   jax.ShapeDtypeStruct((m, tn), F32),
                   jax.ShapeDtypeStruct((m, LANES), F32)],
        scratch_shapes=[pltpu.VMEM((tm, d), BF16)],
        compiler_params=_cparams(("arbitrary", "arbitrary")),
        name="inproj",
    )(x, sh, sc, w_main, w_small)


def _outproj_kernel(a_ref, m_ref, c_ref, x_ref, g_ref, w_ref, lg_ref, lb_ref, o_ref, *, alpha):
    y = _dot(a_ref[...], w_ref[0:A_WIDTH, :])
    y += _dot(m_ref[...], w_ref[A_WIDTH:A_WIDTH + M_WIDTH, :])
    y += _dot(c_ref[...], w_ref[A_WIDTH + M_WIDTH:, :])
    z = alpha * x_ref[...] + (1.0 + g_ref[0]) * y
    o_ref[...] = _normalize(z) * lg_ref[...] + lb_ref[...]


def _outproj(attn, mls, cnv, x, gate, w_out, ln_g, ln_b, per_row, rows_per_batch, tm, alpha):
    m, d = x.shape
    row = lambda w: pl.BlockSpec((tm, w), lambda i: (i, 0))
    vec = pl.BlockSpec((1, d), lambda i: (0, 0))
    return pl.pallas_call(
        functools.partial(_outproj_kernel, alpha=alpha),
        grid=(m // tm,),
        in_specs=[row(A_WIDTH), row(M_WIDTH), row(CONV_CH), row(d),
                  _mod_spec(per_row, tm, rows_per_batch, d),
                  pl.BlockSpec(w_out.shape, lambda i: (0, 0)), vec, vec],
        out_specs=row(d),
        out_shape=jax.ShapeDtypeStruct((m, d), F32),
        compiler_params=_cparams(("arbitrary",)),
        name="outproj",
    )(attn, mls, cnv, x, gate, w_out, ln_g.reshape(1, d), ln_b.reshape(1, d))


def _ffn_kernel(x_ref, sh_ref, sc_ref, g_ref, w1_ref, w2_ref, lg_ref, lb_ref, o_ref, h_scr, *, alpha):
    j = pl.program_id(1)

    @pl.when(j == 0)
    def _():
        h = _normalize(x_ref[...]) * (1.0 + sc_ref[0]) + sh_ref[0]
        h_scr[...] = h.astype(BF16)
        o_ref[...] = jnp.zeros_like(o_ref)

    u = jnp.maximum(_dot(h_scr[...], w1_ref[...]), 0.0)
    o_ref[...] += _dot((u * u).astype(BF16), w2_ref[...])

    @pl.when(j == pl.num_programs(1) - 1)
    def _():
        z = alpha * x_ref[...] + (1.0 + g_ref[0]) * o_ref[...]
        o_ref[...] = _normalize(z) * lg_ref[...] + lb_ref[...]


def _ffn(x, sh, sc, gate, w1, w2, ln_g, ln_b, per_row, rows_per_batch, tm, alpha):
    m, d = x.shape
    dff = w1.shape[1]
    tf = 512
    mod = _mod_spec(per_row, tm, rows_per_batch, d)
    vec = pl.BlockSpec((1, d), lambda i, j: (0, 0))
    return pl.pallas_call(
        functools.partial(_ffn_kernel, alpha=alpha),
        grid=(m // tm, dff // tf),
        in_specs=[pl.BlockSpec((tm, d), lambda i, j: (i, 0)), mod, mod, mod,
                  pl.BlockSpec((d, tf), lambda i, j: (0, j)),
                  pl.BlockSpec((tf, d), lambda i, j: (j, 0)), vec, vec],
        out_specs=pl.BlockSpec((tm, d), lambda i, j: (i, 0)),
        out_shape=jax.ShapeDtypeStruct((m, d), F32),
        scratch_shapes=[pltpu.VMEM((tm, d), BF16)],
        compiler_params=_cparams(("arbitrary", "arbitrary")),
        name="ffn",
    )(x, sh, sc, gate, w1, w2, ln_g.reshape(1, d), ln_b.reshape(1, d))


def _dsa_kernel(q_ref, qi_ref, wi_ref, k_ref, v_ref, ki_ref, o_ref,
                key_scr, qs_scr, qis_scr, s_scr, m_scr, l_scr, acc_scr,
                *, tq, ts, q_off, n_keys, topk, n_tiles_static):
    qb = pl.program_id(1)
    q_pos0 = q_off + qb * tq
    if n_tiles_static is None:
        vis_max = jnp.minimum(((q_pos0 + tq - 1) // CHUNK + 1) * CHUNK, n_keys)
        n_tiles = (vis_max + ts - 1) // ts
    else:
        n_tiles = n_tiles_static
    lane_tiles = [slice(c * LANES, (c + 1) * LANES) for c in range(ts // LANES)]
    sub_tiles = [slice(r * 8, (r + 1) * 8) for r in range(ts // 8)]

    q_chunk = (q_pos0 + lax.broadcasted_iota(jnp.int32, (1, tq), 1)) // CHUNK
    n_vis = jnp.minimum((q_chunk + 1) * CHUNK, n_keys)
    kk = jnp.minimum(topk, n_vis).astype(F32)

    eye = (lax.broadcasted_iota(jnp.int32, (tq, tq), 0)
           == lax.broadcasted_iota(jnp.int32, (tq, tq), 1)).astype(BF16)
    for h in range(A_HEADS):
        rows = slice((h % A_GROUP) * tq, (h % A_GROUP + 1) * tq)
        qs_scr[h // A_GROUP, rows, 0:A_HEAD_DIM] = q_ref[0, :, h * A_HEAD_DIM:(h + 1) * A_HEAD_DIM]
        qs_scr[h // A_GROUP, rows, A_HEAD_DIM:A_HEAD_DIM + tq] = eye
    for h in range(IDX_HEADS):
        qis_scr[h // 2, (h % 2) * tq:(h % 2 + 1) * tq, :] = qi_ref[0, :, h * IDX_DIM:(h + 1) * IDX_DIM]
    wi = wi_ref[0] * IDX_HEADS ** -0.5 * IDX_DIM ** -0.5

    def score_tile(t, carry):
        start = pl.multiple_of(t * ts, ts)
        kt = ki_ref[0, pl.ds(start, ts), :]
        score = None
        for pair in range(IDX_HEADS // 2):
            d = _dot_nt(kt, qis_scr[pair])
            for e in range(2):
                h = 2 * pair + e
                term = wi[h:h + 1, :] * jnp.maximum(d[:, e * tq:(e + 1) * tq], 0.0)
                score = term if score is None else score + term
        score = jnp.where(score == 0.0, 0.0, score)
        bits = lax.bitcast_convert_type(score, jnp.int32)
        key = bits ^ ((bits >> 31) & 0x7FFFFFFF)
        visible = lax.broadcasted_iota(jnp.int32, (ts, tq), 0) < n_vis - start
        key_scr[t] = jnp.where(visible, key, INT_MIN)
        return carry

    lax.fori_loop(0, n_tiles, score_tile, 0)

    n_acc = 8

    def count(pred):
        def body(t, accs):
            accs = list(accs)
            for r, ss in enumerate(sub_tiles):
                accs[r % n_acc] = accs[r % n_acc] + jnp.where(pred(key_scr[t, ss, :]), 1.0, 0.0)
            return tuple(accs)
        accs = lax.fori_loop(0, n_tiles, body, tuple(jnp.zeros((8, tq), F32) for _ in range(n_acc)))
        total = accs[0]
        for a in accs[1:]:
            total = total + a
        return jnp.sum(total, axis=0, keepdims=True)

    def bisect(i, t_u):
        trial = t_u | lax.shift_left(jnp.int32(1), 31 - i)
        cand = jnp.broadcast_to(trial ^ INT_MIN, (8, tq))
        c = count(lambda blk: blk >= cand)
        return jnp.where(c >= kk, trial, t_u)

    t_u = lax.fori_loop(0, 32, bisect, jnp.zeros((1, tq), jnp.int32))
    thr = jnp.broadcast_to(t_u ^ INT_MIN, (8, tq))

    n_ge = count(lambda blk: blk >= thr)

    @pl.when(jnp.max(n_ge - kk) > 0.0)
    def _():
        n_gt = count(lambda blk: blk > thr)
        keep = kk - n_gt
        lower = (lax.broadcasted_iota(jnp.int32, (ts, ts), 0)
                 >= lax.broadcasted_iota(jnp.int32, (ts, ts), 1)).astype(BF16)

        def tie_tile(t, seen):
            blk = key_scr[t]
            eq = blk == thr[0:1, :]
            rank = seen + _dot(lower, jnp.where(eq, 1.0, 0.0).astype(BF16))
            key_scr[t] = jnp.where(eq & (rank > keep), INT_MIN, blk)
            return rank[ts - 1:ts, :]

        lax.fori_loop(0, n_tiles, tie_tile, jnp.zeros((1, tq), F32))

    m_scr[...] = jnp.full(m_scr.shape, NEG_INF, F32)
    l_scr[...] = jnp.zeros(l_scr.shape, F32)
    acc_scr[...] = jnp.zeros(acc_scr.shape, F32)

    def logits_tile(t, slot):
        start = pl.multiple_of(t * ts, ts)
        bias = jnp.where(key_scr[t] >= thr[0:1, :], 0.0, NEG_INF).astype(BF16)
        for n in range(A_KV_HEADS):
            kt = k_ref[0, pl.ds(start, ts), n * A_HEAD_DIM:(n + 1) * A_HEAD_DIM]
            s_scr[slot, n] = _dot_nt(qs_scr[n], jnp.concatenate([kt, bias], axis=1))

    def softmax_pv_tile(t, slot):
        start = pl.multiple_of(t * ts, ts)
        for n in range(A_KV_HEADS):
            vt = v_ref[0, pl.ds(start, ts), n * A_HEAD_DIM:(n + 1) * A_HEAD_DIM]
            for g in range(A_GROUP):
                h = n * A_GROUP + g
                s = [s_scr[slot, n, g * tq:(g + 1) * tq, ls] for ls in lane_tiles]
                mx = s[0]
                for sc in s[1:]:
                    mx = jnp.maximum(mx, sc)
                m_old = m_scr[h]
                m_new = jnp.maximum(m_old, jnp.max(mx, axis=1, keepdims=True))
                alpha = jnp.exp2(m_old - m_new)
                p = [jnp.exp2(sc - m_new) for sc in s]
                psum = p[0]
                for pc in p[1:]:
                    psum = psum + pc
                l_scr[h] = alpha * l_scr[h] + psum
                pv = _dot(jnp.concatenate(p, axis=1).astype(BF16), vt)
                acc_scr[h] = alpha * acc_scr[h] + pv
                m_scr[h] = m_new

    def attend_pair(i, carry):
        t = 2 * i
        logits_tile(t + 1, 1)
        softmax_pv_tile(t, 0)
        logits_tile(t + 2, 0)
        softmax_pv_tile(t + 1, 1)
        return carry

    n_pairs = (n_tiles - 1) // 2
    logits_tile(0, 0)
    lax.fori_loop(0, n_pairs, attend_pair, 0)
    t_rest = 2 * n_pairs
    softmax_pv_tile(t_rest, 0)

    @pl.when(t_rest + 1 < n_tiles)
    def _():
        logits_tile(t_rest + 1, 1)
        softmax_pv_tile(t_rest + 1, 1)

    for h in range(A_HEADS):
        l_tot = jnp.sum(l_scr[h], axis=1, keepdims=True)
        o_ref[0, :, h * A_HEAD_DIM:(h + 1) * A_HEAD_DIM] = (acc_scr[h] / l_tot).astype(BF16)


def _dsa(main3, keys_src, ki, wi, *, k_col, v_col, q_off, n_keys, tq, ts):
    assert tq == A_HEAD_DIM, "the mask rides in the unused half of a 2*A_HEAD_DIM-deep contraction"
    b, t_q, _ = main3.shape
    s_len = keys_src.shape[1]
    topk = min(TOPK_MAX, n_keys // 4)
    dynamic = q_off == 0
    kern = functools.partial(_dsa_kernel, tq=tq, ts=ts, q_off=q_off, n_keys=n_keys, topk=topk,
                             n_tiles_static=None if dynamic else s_len // ts)
    return pl.pallas_call(
        kern,
        grid=(b, t_q // tq),
        in_specs=[pl.BlockSpec((1, tq, A_WIDTH), lambda i, j: (i, j, COL_Q // A_WIDTH)),
                  pl.BlockSpec((1, tq, IDX_HEADS * IDX_DIM), lambda i, j: (i, j, COL_IQ // (IDX_HEADS * IDX_DIM))),
                  pl.BlockSpec((1, IDX_HEADS, tq), lambda i, j: (i, 0, j)),
                  pl.BlockSpec((1, s_len, KV_WIDTH), lambda i, j: (i, 0, k_col)),
                  pl.BlockSpec((1, s_len, KV_WIDTH), lambda i, j: (i, 0, v_col)),
                  pl.BlockSpec((1, s_len, IDX_DIM), lambda i, j: (i, 0, 0))],
        out_specs=pl.BlockSpec((1, tq, A_WIDTH), lambda i, j: (i, j, 0)),
        out_shape=jax.ShapeDtypeStruct((b, t_q, A_WIDTH), BF16),
        scratch_shapes=[pltpu.VMEM((s_len // ts, ts, tq), jnp.int32),
                        pltpu.VMEM((A_KV_HEADS, A_GROUP * tq, A_HEAD_DIM + tq), BF16),
                        pltpu.VMEM((IDX_HEADS // 2, 2 * tq, IDX_DIM), BF16),
                        pltpu.VMEM((2, A_KV_HEADS, A_GROUP * tq, ts), F32),
                        pltpu.VMEM((A_HEADS, tq, LANES), F32),
                        pltpu.VMEM((A_HEADS, tq, LANES), F32),
                        pltpu.VMEM((A_HEADS, tq, A_HEAD_DIM), F32)],
        compiler_params=_cparams(("arbitrary", "arbitrary")),
        name="dsa",
    )(main3, main3, wi, keys_src, keys_src, ki)


def _log_sigmoid(x):
    return jnp.minimum(x, 0.0) - jnp.log1p(jnp.exp(-jnp.abs(x)))


def _mlstm_kernel(q_ref, k_ref, v_ref, o_ref, g_ref, gb_ref, ng_ref, c0_ref, n0_ref, m0_ref,
                  y_ref, c_out, n_out, m_out, c_scr, n_scr, m_scr, *, chunk, valid_len):
    step = pl.program_id(1)

    @pl.when(step == 0)
    def _():
        c_scr[...] = c0_ref[0]
        n_scr[...] = n0_ref[0]
        m_scr[...] = m0_ref[0]

    L = chunk
    g = g_ref[0] + gb_ref[...]
    lf = _log_sigmoid(g)
    if valid_len < L:
        live = lax.broadcasted_iota(jnp.int32, (L, 1), 0) < valid_len
        lf = jnp.where(live, lf, 0.0)
        g = jnp.where(live, g, NEG_INF)
    g_t = g.T
    lf_t = lf.T
    row = lax.broadcasted_iota(jnp.int32, (L, L), 0)
    col = lax.broadcasted_iota(jnp.int32, (L, L), 1)
    lower = row >= col
    b_c = jnp.dot(jnp.where(lower, 1.0, 0.0), lf, precision=lax.Precision.HIGHEST,
                  preferred_element_type=F32)
    b_r = jnp.dot(lf_t, jnp.where(lower, 0.0, 1.0) + jnp.where(row == col, 1.0, 0.0),
                  precision=lax.Precision.HIGHEST, preferred_element_type=F32)

    for h in range(M_HEADS):
        sl = slice(h * M_DIM, (h + 1) * M_DIM)
        b_col = b_c[:, SM_MF + h:SM_MF + h + 1]
        ig_col = g[:, SM_MI + h:SM_MI + h + 1]
        a_row = g_t[SM_MI + h:SM_MI + h + 1, :] - b_r[SM_MF + h:SM_MF + h + 1, :]
        m_prev = m_scr[h, 0:1, 0:1]
        d_mat = jnp.where(lower, b_col + a_row, NEG_INF)
        inter = b_col + m_prev
        m_t = jnp.maximum(inter, jnp.max(d_mat, axis=1, keepdims=True))
        w_intra = jnp.exp(d_mat - m_t)
        w_inter = jnp.exp(inter - m_t)
        qh = q_ref[0, :, sl]
        kf = k_ref[0, :, sl].astype(F32) * M_DIM ** -0.5
        vh = v_ref[0, :, sl]
        s = _dot_nt(qh, kf.astype(BF16)) * w_intra
        c_old = c_scr[h]
        n_old = n_scr[h, 0:1, :]
        num = w_inter * _dot(qh, c_old.astype(BF16)) + _dot(s.astype(BF16), vh)
        den = (w_inter * jnp.sum(qh.astype(F32) * n_old, axis=1, keepdims=True)
               + jnp.sum(s, axis=1, keepdims=True))
        hid = num / jnp.maximum(jnp.abs(den), jnp.exp(-m_t))
        m_end = m_t[L - 1:L, :]
        b_last = b_col[L - 1:L, :]
        w_end = jnp.exp(b_last - b_col + ig_col - m_end)
        decay = jnp.exp(b_last + m_prev - m_end)
        kw = w_end * kf
        c_scr[h] = decay * c_old + _dot(kw.T.astype(BF16), vh)
        n_scr[h] = jnp.broadcast_to(decay * n_old + jnp.sum(kw, axis=0, keepdims=True), (8, M_DIM))
        m_scr[h] = jnp.broadcast_to(m_end, (8, LANES))
        gated = _sigmoid(o_ref[0, :, sl].astype(F32)) * hid
        y_ref[0, :, sl] = (_normalize(gated) * ng_ref[:, sl]).astype(BF16)

    c_out[0] = c_scr[...]
    n_out[0] = n_scr[...]
    m_out[0] = m_scr[...]


def _mlstm(main3, small3, gate_bias, norm_g, c0, n0, m0, *, chunk, valid_len):
    b, t, _ = main3.shape
    blk = lambda col: pl.BlockSpec((1, chunk, M_WIDTH), lambda i, j: (i, j, col // M_WIDTH))
    st = lambda shape: pl.BlockSpec((1,) + shape, lambda i, j: (i,) + (0,) * len(shape))
    c_shape, n_shape, m_shape = (M_HEADS, M_DIM, M_DIM), (M_HEADS, 8, M_DIM), (M_HEADS, 8, LANES)
    return pl.pallas_call(
        functools.partial(_mlstm_kernel, chunk=chunk, valid_len=valid_len),
        grid=(b, t // chunk),
        in_specs=[blk(COL_MQ), blk(COL_MK), blk(COL_MV), blk(COL_MO),
                  pl.BlockSpec((1, chunk, LANES), lambda i, j: (i, j, 0)),
                  pl.BlockSpec((1, LANES), lambda i, j: (0, 0)),
                  pl.BlockSpec((1, M_WIDTH), lambda i, j: (0, 0)),
                  st(c_shape), st(n_shape), st(m_shape)],
        out_specs=[pl.BlockSpec((1, chunk, M_WIDTH), lambda i, j: (i, j, 0)),
                   st(c_shape), st(n_shape), st(m_shape)],
        out_shape=[jax.ShapeDtypeStruct((b, t, M_WIDTH), BF16),
                   jax.ShapeDtypeStruct((b,) + c_shape, F32),
                   jax.ShapeDtypeStruct((b,) + n_shape, F32),
                   jax.ShapeDtypeStruct((b,) + m_shape, F32)],
        scratch_shapes=[pltpu.VMEM(c_shape, F32), pltpu.VMEM(n_shape, F32), pltpu.VMEM(m_shape, F32)],
        compiler_params=_cparams(("arbitrary", "arbitrary")),
        name="mlstm",
    )(main3, main3, main3, main3, small3, gate_bias, norm_g.reshape(1, M_WIDTH), c0, n0, m0)


def _conv_kernel(cu_ref, halo_ref, st_ref, w_ref, cb_ref, ng_ref, nb_ref, y_ref, ns_ref, full_scr,
                 *, tm, valid_len):
    i = pl.program_id(1)

    def glu(u):
        return u[:, :CONV_CH].astype(F32) * _sigmoid(u[:, CONV_CH:].astype(F32))

    @pl.when(i == 0)
    def _():
        full_scr[0:CONV_HALO, :] = st_ref[0]

    @pl.when(i > 0)
    def _():
        full_scr[0:CONV_HALO, :] = glu(halo_ref[0])

    full_scr[CONV_HALO:CONV_HALO + tm, :] = glu(cu_ref[0])
    acc = jnp.zeros((tm, CONV_CH), F32)
    for j in range(CONV_W):
        acc = acc + w_ref[j:j + 1, :] * full_scr[pl.ds(CONV_PAD + j, tm), :]
    y = _normalize(acc + cb_ref[...]) * ng_ref[...] + nb_ref[...]
    y_ref[0] = (y * _sigmoid(y)).astype(BF16)
    ns_ref[0] = full_scr[pl.ds(valid_len, CONV_HALO), :]


def _conv(main3, state, w, bias, ng, nb, *, tm, valid_len):
    b, t, _ = main3.shape
    cu_blk = COL_CU // (2 * CONV_CH)
    hpb = tm // CONV_HALO
    vec = pl.BlockSpec((1, CONV_CH), lambda i, j: (0, 0))
    return pl.pallas_call(
        functools.partial(_conv_kernel, tm=tm, valid_len=valid_len),
        grid=(b, t // tm),
        in_specs=[pl.BlockSpec((1, tm, 2 * CONV_CH), lambda i, j: (i, j, cu_blk)),
                  pl.BlockSpec((1, CONV_HALO, 2 * CONV_CH), lambda i, j: (i, jnp.maximum(j * hpb - 1, 0), cu_blk)),
                  pl.BlockSpec((1, CONV_HALO, CONV_CH), lambda i, j: (i, 0, 0)),
                  pl.BlockSpec((CONV_W, CONV_CH), lambda i, j: (0, 0)), vec, vec, vec],
        out_specs=[pl.BlockSpec((1, tm, CONV_CH), lambda i, j: (i, j, 0)),
                   pl.BlockSpec((1, CONV_HALO, CONV_CH), lambda i, j: (i, 0, 0))],
        out_shape=[jax.ShapeDtypeStruct((b, t, CONV_CH), BF16),
                   jax.ShapeDtypeStruct((b, CONV_HALO, CONV_CH), F32)],
        scratch_shapes=[pltpu.VMEM((CONV_HALO + tm, CONV_CH), F32)],
        compiler_params=_cparams(("arbitrary", "arbitrary")),
        name="conv",
    )(main3, main3, state, w, bias.reshape(1, CONV_CH), ng.reshape(1, CONV_CH), nb.reshape(1, CONV_CH))


def _split_w_in(w_in):
    sizes = (A_WIDTH, KV_WIDTH, KV_WIDTH, IDX_HEADS * IDX_DIM, IDX_DIM, IDX_HEADS,
             M_WIDTH, M_WIDTH, M_WIDTH, M_HEADS, M_HEADS, M_WIDTH, 2 * CONV_CH)
    parts, start = [], 0
    for size in sizes:
        parts.append(w_in[:, start:start + size])
        start += size
    aq, ak, av, iq, ik, iw, mq, mk, mv, mi, mf, mo, cu = parts
    aq = aq * ATTN_LOG2E_SCALE
    w_main = jnp.concatenate([aq, ak, av, iq, mq, mk, mv, mo, cu], axis=1).astype(BF16)
    pad = jnp.zeros((w_in.shape[0], LANES - SM_USED), w_in.dtype)
    w_small = jnp.concatenate([ik, iw, mi, mf, pad], axis=1).astype(BF16)
    return w_main, w_small


def _tiles(m, rows_per_batch, per_row):
    if per_row:
        return dict(inproj=m, outproj=m, ffn=m, dsa_q=128, dsa_k=512, mlstm=128, conv=128)
    t = rows_per_batch
    return dict(inproj=min(1024, t), outproj=min(512, t), ffn=min(1024, t),
                dsa_q=min(128, t), dsa_k=min(512, t), mlstm=min(256, t), conv=min(512, t))


def _layer(x, ada, p, cache, *, rows_per_batch, alpha):
    m, d = x.shape
    nb = m // rows_per_batch
    t = rows_per_batch
    per_row = cache is not None
    tiles = _tiles(m, t, per_row)
    if per_row:
        mods = [jnp.repeat(v, t, axis=0)[None] for v in jnp.split(ada, 6, axis=-1)]
    else:
        mods = [v[:, None, :] for v in jnp.split(ada, 6, axis=-1)]
    sh1, sc1, g1, sh2, sc2, g2 = mods

    main, kv, small = _inproj(x, sh1, sc1, p['w_main'], p['w_small'], per_row, t, tiles['inproj'])
    k_new = kv[:, :KV_WIDTH].reshape(nb, t, A_KV_HEADS, A_HEAD_DIM)
    v_new = kv[:, KV_WIDTH:].reshape(nb, t, A_KV_HEADS, A_HEAD_DIM)
    ik_new = small[:, SM_IK:SM_IK + IDX_DIM].reshape(nb, t, IDX_DIM)

    gate_bias = jnp.zeros((1, LANES), F32)
    gate_bias = gate_bias.at[0, SM_MI:SM_MI + M_HEADS].set(p['b_igate'])
    gate_bias = gate_bias.at[0, SM_MF:SM_MF + M_HEADS].set(p['b_fgate'])

    if cache is None:
        main3 = main.reshape(nb, t, MAIN_COLS)
        small3 = small.reshape(nb, t, LANES)
        ki = ik_new.astype(BF16)
        wi = jnp.swapaxes(small3[:, :, SM_IW:SM_IW + IDX_HEADS], 1, 2)
        attn = _dsa(main3, main3, ki, wi, k_col=COL_K // KV_WIDTH, v_col=COL_V // KV_WIDTH,
                    q_off=0, n_keys=t, tq=tiles['dsa_q'], ts=tiles['dsa_k'])
        c0 = jnp.zeros((nb, M_HEADS, M_DIM, M_DIM), F32)
        n0 = jnp.zeros((nb, M_HEADS, 8, M_DIM), F32)
        m0 = jnp.zeros((nb, M_HEADS, 8, LANES), F32)
        mls, c_new, n_new, m_new = _mlstm(main3, small3, gate_bias, p['mlstm_norm_g'], c0, n0, m0,
                                          chunk=tiles['mlstm'], valid_len=tiles['mlstm'])
        state = jnp.zeros((nb, CONV_HALO, CONV_CH), F32)
        cnv, conv_new = _conv(main3, state, p['conv_w'], p['conv_b'], p['conv_norm_g'], p['conv_norm_b'],
                              tm=tiles['conv'], valid_len=tiles['conv'])
        attn, mls, cnv = (a.reshape(m, -1) for a in (attn, mls, cnv))
    else:
        ck, cv, cik, c_st, n_st, m_st, conv_st = cache
        past = ck.shape[1]
        n_keys = past + t
        tp = tiles['dsa_q']
        ts = tiles['dsa_k']
        s_pad = -(-n_keys // ts) * ts
        main3 = jnp.pad(main.reshape(nb, t, MAIN_COLS), ((0, 0), (0, tp - t), (0, 0)))
        small3 = jnp.pad(small.reshape(nb, t, LANES), ((0, 0), (0, tp - t), (0, 0)))
        pad_keys = lambda a: jnp.pad(a, ((0, 0), (0, s_pad - n_keys), (0, 0)))
        k_all = pad_keys(jnp.concatenate([ck.reshape(nb, past, KV_WIDTH), kv[:, :KV_WIDTH].reshape(nb, t, KV_WIDTH)], axis=1))
        v_all = pad_keys(jnp.concatenate([cv.reshape(nb, past, KV_WIDTH), kv[:, KV_WIDTH:].reshape(nb, t, KV_WIDTH)], axis=1))
        kv_all = jnp.concatenate([k_all, v_all], axis=-1).astype(BF16)
        ki = pad_keys(jnp.concatenate([cik, ik_new], axis=1)).astype(BF16)
        wi = jnp.swapaxes(small3[:, :, SM_IW:SM_IW + IDX_HEADS], 1, 2)
        attn = _dsa(main3, kv_all, ki, wi, k_col=0, v_col=1, q_off=past, n_keys=n_keys, tq=tp, ts=ts)
        n0 = jnp.broadcast_to(n_st[:, :, None, :], (nb, M_HEADS, 8, M_DIM))
        m0 = jnp.broadcast_to(m_st[:, :, None, None], (nb, M_HEADS, 8, LANES))
        mls, c_new, n_new, m_new = _mlstm(main3, small3, gate_bias, p['mlstm_norm_g'], c_st, n0, m0,
                                          chunk=tp, valid_len=t)
        state = jnp.pad(conv_st, ((0, 0), (CONV_PAD, 0), (0, 0)))
        cnv, conv_new = _conv(main3, state, p['conv_w'], p['conv_b'], p['conv_norm_g'], p['conv_norm_b'],
                              tm=tp, valid_len=t)
        attn, mls, cnv = (a[:, :t].reshape(m, -1) for a in (attn, mls, cnv))

    x = _outproj(attn, mls, cnv, x, g1, p['w_out'], p['ln1_g'], p['ln1_b'], per_row, t, tiles['outproj'], alpha)
    x = _ffn(x, sh2, sc2, g2, p['w_ff1'], p['w_ff2'], p['ln2_g'], p['ln2_b'], per_row, t, tiles['ffn'], alpha)
    new_state = (k_new, v_new, ik_new, c_new, n_new[:, :, 0, :], m_new[:, :, 0, 0], conv_new[:, CONV_PAD:, :])
    return x, new_state


def kernel(x_prompt, x_sample, c_prompt, c_sample, cache_attn_k, cache_attn_v, cache_idx_k,
           state_mlstm_C, state_mlstm_n, state_mlstm_m, state_conv,
           w_ada, b_ada, w_in, b_igate, b_fgate, mlstm_norm_g, conv_w, conv_b,
           conv_norm_g, conv_norm_b, w_out, ln1_g, ln1_b, w_ff1, w_ff2, ln2_g, ln2_b):
    depth = w_ada.shape[0]
    alpha = (2 * depth) ** 0.25
    bp, tp, d = x_prompt.shape
    bs, tsmp, _ = x_sample.shape

    c_all = jnp.concatenate([c_prompt, c_sample], axis=0)
    rows = -(-c_all.shape[0] // 8) * 8
    ada_all = _ada(jnp.pad(c_all, ((0, rows - c_all.shape[0]), (0, 0))), w_ada, b_ada)

    xp = x_prompt.reshape(bp * tp, d)
    xs = x_sample.reshape(bs * tsmp, d)
    st_p, st_s = [], []
    for l in range(depth):
        w_main, w_small = _split_w_in(w_in[l])
        p = {'w_main': w_main, 'w_small': w_small, 'b_igate': b_igate[l], 'b_fgate': b_fgate[l],
             'mlstm_norm_g': mlstm_norm_g[l], 'conv_w': conv_w[l], 'conv_b': conv_b[l],
             'conv_norm_g': conv_norm_g[l], 'conv_norm_b': conv_norm_b[l],
             'w_out': w_out[l].astype(BF16), 'ln1_g': ln1_g[l], 'ln1_b': ln1_b[l],
             'w_ff1': w_ff1[l].astype(BF16), 'w_ff2': w_ff2[l].astype(BF16),
             'ln2_g': ln2_g[l], 'ln2_b': ln2_b[l]}
        xp, sp = _layer(xp, ada_all[l, :bp], p, None, rows_per_batch=tp, alpha=alpha)
        cache = (cache_attn_k[l], cache_attn_v[l], cache_idx_k[l], state_mlstm_C[l],
                 state_mlstm_n[l], state_mlstm_m[l], state_conv[l])
        xs, ss = _layer(xs, ada_all[l, bp:bp + bs], p, cache, rows_per_batch=tsmp, alpha=alpha)
        st_p.append(sp)
        st_s.append(ss)

    def stack(sts, i):
        return jnp.stack([s[i] for s in sts], axis=0)

    return (xp.reshape(bp, tp, d), xs.reshape(bs, tsmp, d),
            *(stack(st_p, i) for i in range(7)), *(stack(st_s, i) for i in range(7)))
```

```python
import functools

import jax
import jax.numpy as jnp
from jax import lax
from jax.experimental import pallas as pl
from jax.experimental.pallas import tpu as pltpu

F32 = jnp.float32
BF16 = jnp.bfloat16

CHUNK = 64
A_HEADS = 8
A_KV_HEADS = 2
A_GROUP = A_HEADS // A_KV_HEADS
A_HEAD_DIM = 128
A_WIDTH = A_HEADS * A_HEAD_DIM
KV_WIDTH = A_KV_HEADS * A_HEAD_DIM
IDX_HEADS = 8
IDX_DIM = 64
TOPK_MAX = 256
M_HEADS = 4
M_DIM = 128
M_WIDTH = M_HEADS * M_DIM
CONV_CH = 512
CONV_W = 31
LN_EPS = 1e-5
NEG_INF = -1e30
ATTN_LOG2E_SCALE = A_HEAD_DIM ** -0.5 * 1.4426950408889634
INT_MIN = -2 ** 31

LANES = 128
VMEM_LIMIT_BYTES = 56 * 1024 * 1024

COL_Q = 0
COL_K = COL_Q + A_WIDTH
COL_V = COL_K + KV_WIDTH
COL_IQ = COL_V + KV_WIDTH
COL_MQ = COL_IQ + IDX_HEADS * IDX_DIM
COL_MK = COL_MQ + M_WIDTH
COL_MV = COL_MK + M_WIDTH
COL_MO = COL_MV + M_WIDTH
COL_CU = COL_MO + M_WIDTH
MAIN_COLS = COL_CU + 2 * CONV_CH
SM_IK = 0
SM_IW = SM_IK + IDX_DIM
SM_MI = SM_IW + IDX_HEADS
SM_MF = SM_MI + M_HEADS
SM_USED = SM_MF + M_HEADS

CONV_HALO = 32
CONV_PAD = CONV_HALO - (CONV_W - 1)


def _cparams(sem):
    return pltpu.CompilerParams(dimension_semantics=sem, vmem_limit_bytes=VMEM_LIMIT_BYTES)


def _normalize(x):
    mu = jnp.mean(x, axis=-1, keepdims=True)
    xc = x - mu
    var = jnp.mean(xc * xc, axis=-1, keepdims=True)
    return xc * lax.rsqrt(var + LN_EPS)


def _sigmoid(x):
    return 1.0 / (1.0 + jnp.exp(-x))


def _dot(a, b):
    return jnp.dot(a, b, preferred_element_type=F32)


def _dot_nt(a, b):
    return lax.dot_general(a, b, (((1,), (1,)), ((), ())), preferred_element_type=F32)


def _ada_kernel(c_ref, w_ref, b_ref, o_ref):
    c = c_ref[...]
    s = (c * _sigmoid(c)).astype(BF16)
    o_ref[0] = _dot(s, w_ref[0].astype(BF16)) + b_ref[0]


def _ada(c_all, w_ada, b_ada):
    depth, d, n = w_ada.shape
    rows = c_all.shape[0]
    tn = 1024
    return pl.pallas_call(
        _ada_kernel,
        grid=(depth, n // tn),
        in_specs=[pl.BlockSpec((rows, d), lambda l, j: (0, 0)),
                  pl.BlockSpec((1, d, tn), lambda l, j: (l, 0, j)),
                  pl.BlockSpec((1, 1, tn), lambda l, j: (l, 0, j))],
        out_specs=pl.BlockSpec((1, rows, tn), lambda l, j: (l, 0, j)),
        out_shape=jax.ShapeDtypeStruct((depth, rows, n), F32),
        compiler_params=_cparams(("arbitrary", "arbitrary")),
        name="ada",
    )(c_all, w_ada, b_ada.reshape(depth, 1, n))


def _mod_spec(per_row, tm, rows_per_batch, d):
    if per_row:
        return pl.BlockSpec((1, tm, d), lambda i, *_: (0, i, 0))
    bpb = rows_per_batch // tm
    return pl.BlockSpec((1, 1, d), lambda i, *_: (i // bpb, 0, 0))


def _inproj_kernel(x_ref, sh_ref, sc_ref, wm_ref, ws_ref, main_ref, kv_ref, small_ref, h_scr, *, kv_block):
    j = pl.program_id(1)

    @pl.when(j == 0)
    def _():
        h = _normalize(x_ref[...]) * (1.0 + sc_ref[0]) + sh_ref[0]
        hb = h.astype(BF16)
        h_scr[...] = hb
        small_ref[...] = _dot(hb, ws_ref[...])

    acc = _dot(h_scr[...], wm_ref[...])
    main_ref[...] = acc.astype(BF16)

    @pl.when(j == kv_block)
    def _():
        kv_ref[...] = acc


def _inproj(x, sh, sc, w_main, w_small, per_row, rows_per_batch, tm):
    m, d = x.shape
    tn = 2 * KV_WIDTH
    mod = _mod_spec(per_row, tm, rows_per_batch, d)
    return pl.pallas_call(
        functools.partial(_inproj_kernel, kv_block=COL_K // tn),
        grid=(m // tm, MAIN_COLS // tn),
        in_specs=[pl.BlockSpec((tm, d), lambda i, j: (i, 0)), mod, mod,
                  pl.BlockSpec((d, tn), lambda i, j: (0, j)),
                  pl.BlockSpec((d, LANES), lambda i, j: (0, 0))],
        out_specs=[pl.BlockSpec((tm, tn), lambda i, j: (i, j)),
                   pl.BlockSpec((tm, tn), lambda i, j: (i, 0)),
                   pl.BlockSpec((tm, LANES), lambda i, j: (i, 0))],
        out_shape=[jax.ShapeDtypeStruct((m, MAIN_COLS), BF16),
                   jax.ShapeDtypeStruct((m, tn), F32),
                   jax.ShapeDtypeStruct((m, LANES), F32)],
        scratch_shapes=[pltpu.VMEM((tm, d), BF16)],
        compiler_params=_cparams(("arbitrary", "arbitrary")),
        name="inproj",
    )(x, sh, sc, w_main, w_small)


def _outproj_kernel(a_ref, m_ref, c_ref, x_ref, g_ref, w_ref, lg_ref, lb_ref, o_ref, *, alpha):
    y = _dot(a_ref[...], w_ref[0:A_WIDTH, :])
    y += _dot(m_ref[...], w_ref[A_WIDTH:A_WIDTH + M_WIDTH, :])
    y += _dot(c_ref[...], w_ref[A_WIDTH + M_WIDTH:, :])
    z = alpha * x_ref[...] + (1.0 + g_ref[0]) * y
    o_ref[...] = _normalize(z) * lg_ref[...] + lb_ref[...]


def _outproj(attn, mls, cnv, x, gate, w_out, ln_g, ln_b, per_row, rows_per_batch, tm, alpha):
    m, d = x.shape
    row = lambda w: pl.BlockSpec((tm, w), lambda i: (i, 0))
    vec = pl.BlockSpec((1, d), lambda i: (0, 0))
    return pl.pallas_call(
        functools.partial(_outproj_kernel, alpha=alpha),
        grid=(m // tm,),
        in_specs=[row(A_WIDTH), row(M_WIDTH), row(CONV_CH), row(d),
                  _mod_spec(per_row, tm, rows_per_batch, d),
                  pl.BlockSpec(w_out.shape, lambda i: (0, 0)), vec, vec],
        out_specs=row(d),
        out_shape=jax.ShapeDtypeStruct((m, d), F32),
        compiler_params=_cparams(("arbitrary",)),
        name="outproj",
    )(attn, mls, cnv, x, gate, w_out, ln_g.reshape(1, d), ln_b.reshape(1, d))


def _ffn_kernel(x_ref, sh_ref, sc_ref, g_ref, w1_ref, w2_ref, lg_ref, lb_ref, o_ref, h_scr, *, alpha):
    j = pl.program_id(1)

    @pl.when(j == 0)
    def _():
        h = _normalize(x_ref[...]) * (1.0 + sc_ref[0]) + sh_ref[0]
        h_scr[...] = h.astype(BF16)
        o_ref[...] = jnp.zeros_like(o_ref)

    u = jnp.maximum(_dot(h_scr[...], w1_ref[...]), 0.0)
    o_ref[...] += _dot((u * u).astype(BF16), w2_ref[...])

    @pl.when(j == pl.num_programs(1) - 1)
    def _():
        z = alpha * x_ref[...] + (1.0 + g_ref[0]) * o_ref[...]
        o_ref[...] = _normalize(z) * lg_ref[...] + lb_ref[...]


def _ffn(x, sh, sc, gate, w1, w2, ln_g, ln_b, per_row, rows_per_batch, tm, alpha):
    m, d = x.shape
    dff = w1.shape[1]
    tf = 512
    mod = _mod_spec(per_row, tm, rows_per_batch, d)
    vec = pl.BlockSpec((1, d), lambda i, j: (0, 0))
    return pl.pallas_call(
        functools.partial(_ffn_kernel, alpha=alpha),
        grid=(m // tm, dff // tf),
        in_specs=[pl.BlockSpec((tm, d), lambda i, j: (i, 0)), mod, mod, mod,
                  pl.BlockSpec((d, tf), lambda i, j: (0, j)),
                  pl.BlockSpec((tf, d), lambda i, j: (j, 0)), vec, vec],
        out_specs=pl.BlockSpec((tm, d), lambda i, j: (i, 0)),
        out_shape=jax.ShapeDtypeStruct((m, d), F32),
        scratch_shapes=[pltpu.VMEM((tm, d), BF16)],
        compiler_params=_cparams(("arbitrary", "arbitrary")),
        name="ffn",
    )(x, sh, sc, gate, w1, w2, ln_g.reshape(1, d), ln_b.reshape(1, d))


def _dsa_kernel(q_ref, qi_ref, wi_ref, k_ref, v_ref, ki_ref, o_ref,
                key_scr, hi_scr, lo_scr, qs_scr, qis_scr, s_scr, m_scr, l_scr, acc_scr,
                *, tq, ts, q_off, n_keys, topk, n_tiles_static):
    qb = pl.program_id(1)
    q_pos0 = q_off + qb * tq
    if n_tiles_static is None:
        vis_max = jnp.minimum(((q_pos0 + tq - 1) // CHUNK + 1) * CHUNK, n_keys)
        n_tiles = (vis_max + ts - 1) // ts
    else:
        n_tiles = n_tiles_static
    lane_tiles = [slice(c * LANES, (c + 1) * LANES) for c in range(ts // LANES)]
    sub_tiles = [slice(r * 8, (r + 1) * 8) for r in range(ts // 8)]

    q_chunk = (q_pos0 + lax.broadcasted_iota(jnp.int32, (1, tq), 1)) // CHUNK
    n_vis = jnp.minimum((q_chunk + 1) * CHUNK, n_keys)
    kk = jnp.minimum(topk, n_vis).astype(F32)

    eye = (lax.broadcasted_iota(jnp.int32, (tq, tq), 0)
           == lax.broadcasted_iota(jnp.int32, (tq, tq), 1)).astype(BF16)
    for h in range(A_HEADS):
        rows = slice((h % A_GROUP) * tq, (h % A_GROUP + 1) * tq)
        qs_scr[h // A_GROUP, rows, 0:A_HEAD_DIM] = q_ref[0, :, h * A_HEAD_DIM:(h + 1) * A_HEAD_DIM]
        qs_scr[h // A_GROUP, rows, A_HEAD_DIM:A_HEAD_DIM + tq] = eye
    for h in range(IDX_HEADS):
        qis_scr[h // 2, (h % 2) * tq:(h % 2 + 1) * tq, :] = qi_ref[0, :, h * IDX_DIM:(h + 1) * IDX_DIM]
    wi = wi_ref[0] * IDX_HEADS ** -0.5 * IDX_DIM ** -0.5

    def score_tile(t):
        start = pl.multiple_of(t * ts, ts)
        kt = ki_ref[0, pl.ds(start, ts), :]
        score = None
        for pair in range(IDX_HEADS // 2):
            d = _dot_nt(kt, qis_scr[pair])
            for e in range(2):
                h = 2 * pair + e
                term = wi[h:h + 1, :] * jnp.maximum(d[:, e * tq:(e + 1) * tq], 0.0)
                score = term if score is None else score + term
        score = jnp.where(score == 0.0, 0.0, score)
        bits = lax.bitcast_convert_type(score, jnp.int32)
        key = bits ^ ((bits >> 31) & 0x7FFFFFFF)
        visible = lax.broadcasted_iota(jnp.int32, (ts, tq), 0) < n_vis - start
        key = jnp.where(visible, key, INT_MIN)
        key_scr[t] = key
        hi_scr[t] = (key >> 16).astype(jnp.int16)
        lo_scr[t] = (((key ^ 0x8000) << 16) >> 16).astype(jnp.int16)

    def score_pair(i, carry):
        score_tile(2 * i)
        score_tile(2 * i + 1)
        return carry

    lax.fori_loop(0, n_tiles // 2, score_pair, 0)

    @pl.when(n_tiles % 2 == 1)
    def _():
        score_tile(n_tiles - 1)

    n_acc = 8
    pack_tiles = [slice(r * 16, (r + 1) * 16) for r in range(ts // 16)]

    def count16(half_scr, pred):
        def body(t, accs):
            accs = list(accs)
            for r, ss in enumerate(pack_tiles):
                hit = jnp.where(pred(half_scr[t, ss, :]), jnp.int16(1), jnp.int16(0))
                accs[r % n_acc] = accs[r % n_acc] + hit
            return tuple(accs)
        accs = lax.fori_loop(0, n_tiles, body, tuple(jnp.zeros((16, tq), jnp.int16) for _ in range(n_acc)))
        total = accs[0].astype(F32)
        for a in accs[1:]:
            total = total + a.astype(F32)
        return jnp.sum(total, axis=0, keepdims=True)

    def as_packed(x):
        return jnp.broadcast_to(x, (16, tq)).astype(jnp.int16)

    def bisect16(half_scr, want):
        def step(i, t_u):
            trial = t_u | lax.shift_left(jnp.int32(1), 15 - i)
            cand = as_packed(trial - 0x8000)
            c = count16(half_scr, lambda blk: blk >= cand)
            return jnp.where(c >= want, trial, t_u)
        return lax.fori_loop(0, 16, step, jnp.zeros((1, tq), jnp.int32))

    hi_thr = bisect16(hi_scr, kk) - 0x8000
    hi_b = as_packed(hi_thr)
    above = count16(hi_scr, lambda blk: blk > hi_b)

    def restrict_tile(t, carry):
        for ss in pack_tiles:
            lo_scr[t, ss, :] = jnp.where(hi_scr[t, ss, :] == hi_b, lo_scr[t, ss, :], jnp.int16(-0x8000))
        return carry

    lax.fori_loop(0, n_tiles, restrict_tile, 0)
    lo_thr = bisect16(lo_scr, kk - above)
    thr = jnp.broadcast_to(hi_thr * 0x10000 + lo_thr, (8, tq))

    def count(pred):
        def body(t, accs):
            accs = list(accs)
            for r, ss in enumerate(sub_tiles):
                accs[r % n_acc] = accs[r % n_acc] + jnp.where(pred(key_scr[t, ss, :]), 1.0, 0.0)
            return tuple(accs)
        accs = lax.fori_loop(0, n_tiles, body, tuple(jnp.zeros((8, tq), F32) for _ in range(n_acc)))
        total = accs[0]
        for a in accs[1:]:
            total = total + a
        return jnp.sum(total, axis=0, keepdims=True)

    n_ge = count(lambda blk: blk >= thr)

    @pl.when(jnp.max(n_ge - kk) > 0.0)
    def _():
        n_gt = count(lambda blk: blk > thr)
        keep = kk - n_gt
        lower = (lax.broadcasted_iota(jnp.int32, (ts, ts), 0)
                 >= lax.broadcasted_iota(jnp.int32, (ts, ts), 1)).astype(BF16)

        def tie_tile(t, seen):
            blk = key_scr[t]
            eq = blk == thr[0:1, :]
            rank = seen + _dot(lower, jnp.where(eq, 1.0, 0.0).astype(BF16))
            key_scr[t] = jnp.where(eq & (rank > keep), INT_MIN, blk)
            return rank[ts - 1:ts, :]

        lax.fori_loop(0, n_tiles, tie_tile, jnp.zeros((1, tq), F32))

    m_scr[...] = jnp.full(m_scr.shape, NEG_INF, F32)
    l_scr[...] = jnp.zeros(l_scr.shape, F32)
    acc_scr[...] = jnp.zeros(acc_scr.shape, F32)

    def logits_tile(t, slot):
        start = pl.multiple_of(t * ts, ts)
        bias = jnp.where(key_scr[t] >= thr[0:1, :], 0.0, NEG_INF).astype(BF16)
        for n in range(A_KV_HEADS):
            kt = k_ref[0, pl.ds(start, ts), n * A_HEAD_DIM:(n + 1) * A_HEAD_DIM]
            s_scr[slot, n] = _dot_nt(qs_scr[n], jnp.concatenate([kt, bias], axis=1))

    def softmax_pv_tile(t, slot):
        start = pl.multiple_of(t * ts, ts)
        for n in range(A_KV_HEADS):
            vt = v_ref[0, pl.ds(start, ts), n * A_HEAD_DIM:(n + 1) * A_HEAD_DIM]
            for g in range(A_GROUP):
                h = n * A_GROUP + g
                s = [s_scr[slot, n, g * tq:(g + 1) * tq, ls] for ls in lane_tiles]
                mx = s[0]
                for sc in s[1:]:
                    mx = jnp.maximum(mx, sc)
                m_old = m_scr[h]
                m_new = jnp.maximum(m_old, jnp.max(mx, axis=1, keepdims=True))
                alpha = jnp.exp2(m_old - m_new)
                p = [jnp.exp2(sc - m_new) for sc in s]
                psum = p[0]
                for pc in p[1:]:
                    psum = psum + pc
                l_scr[h] = alpha * l_scr[h] + psum
                pv = _dot(jnp.concatenate(p, axis=1).astype(BF16), vt)
                acc_scr[h] = alpha * acc_scr[h] + pv
                m_scr[h] = m_new

    def attend_pair(i, carry):
        t = 2 * i
        logits_tile(t + 1, 1)
        softmax_pv_tile(t, 0)
        logits_tile(t + 2, 0)
        softmax_pv_tile(t + 1, 1)
        return carry

    n_pairs = (n_tiles - 1) // 2
    logits_tile(0, 0)
    lax.fori_loop(0, n_pairs, attend_pair, 0)
    t_rest = 2 * n_pairs
    softmax_pv_tile(t_rest, 0)

    @pl.when(t_rest + 1 < n_tiles)
    def _():
        logits_tile(t_rest + 1, 1)
        softmax_pv_tile(t_rest + 1, 1)

    for h in range(A_HEADS):
        l_tot = jnp.sum(l_scr[h], axis=1, keepdims=True)
        o_ref[0, :, h * A_HEAD_DIM:(h + 1) * A_HEAD_DIM] = (acc_scr[h] / l_tot).astype(BF16)


def _dsa(main3, keys_src, ki, wi, *, k_col, v_col, q_off, n_keys, tq, ts):
    assert tq == A_HEAD_DIM, "the mask rides in the unused half of a 2*A_HEAD_DIM-deep contraction"
    b, t_q, _ = main3.shape
    s_len = keys_src.shape[1]
    topk = min(TOPK_MAX, n_keys // 4)
    dynamic = q_off == 0
    kern = functools.partial(_dsa_kernel, tq=tq, ts=ts, q_off=q_off, n_keys=n_keys, topk=topk,
                             n_tiles_static=None if dynamic else s_len // ts)
    return pl.pallas_call(
        kern,
        grid=(b, t_q // tq),
        in_specs=[pl.BlockSpec((1, tq, A_WIDTH), lambda i, j: (i, j, COL_Q // A_WIDTH)),
                  pl.BlockSpec((1, tq, IDX_HEADS * IDX_DIM), lambda i, j: (i, j, COL_IQ // (IDX_HEADS * IDX_DIM))),
                  pl.BlockSpec((1, IDX_HEADS, tq), lambda i, j: (i, 0, j)),
                  pl.BlockSpec((1, s_len, KV_WIDTH), lambda i, j: (i, 0, k_col)),
                  pl.BlockSpec((1, s_len, KV_WIDTH), lambda i, j: (i, 0, v_col)),
                  pl.BlockSpec((1, s_len, IDX_DIM), lambda i, j: (i, 0, 0))],
        out_specs=pl.BlockSpec((1, tq, A_WIDTH), lambda i, j: (i, j, 0)),
        out_shape=jax.ShapeDtypeStruct((b, t_q, A_WIDTH), BF16),
        scratch_shapes=[pltpu.VMEM((s_len // ts, ts, tq), jnp.int32),
                        pltpu.VMEM((s_len // ts, ts, tq), jnp.int16),
                        pltpu.VMEM((s_len // ts, ts, tq), jnp.int16),
                        pltpu.VMEM((A_KV_HEADS, A_GROUP * tq, A_HEAD_DIM + tq), BF16),
                        pltpu.VMEM((IDX_HEADS // 2, 2 * tq, IDX_DIM), BF16),
                        pltpu.VMEM((2, A_KV_HEADS, A_GROUP * tq, ts), F32),
                        pltpu.VMEM((A_HEADS, tq, LANES), F32),
                        pltpu.VMEM((A_HEADS, tq, LANES), F32),
                        pltpu.VMEM((A_HEADS, tq, A_HEAD_DIM), F32)],
        compiler_params=_cparams(("arbitrary", "arbitrary")),
        name="dsa",
    )(main3, main3, wi, keys_src, keys_src, ki)


def _log_sigmoid(x):
    return jnp.minimum(x, 0.0) - jnp.log1p(jnp.exp(-jnp.abs(x)))


def _mlstm_kernel(q_ref, k_ref, v_ref, o_ref, g_ref, gb_ref, ng_ref, c0_ref, n0_ref, m0_ref,
                  y_ref, c_out, n_out, m_out, c_scr, n_scr, m_scr, *, chunk, valid_len):
    step = pl.program_id(1)

    @pl.when(step == 0)
    def _():
        c_scr[...] = c0_ref[0]
        n_scr[...] = n0_ref[0]
        m_scr[...] = m0_ref[0]

    L = chunk
    g = g_ref[0] + gb_ref[...]
    lf = _log_sigmoid(g)
    if valid_len < L:
        live = lax.broadcasted_iota(jnp.int32, (L, 1), 0) < valid_len
        lf = jnp.where(live, lf, 0.0)
        g = jnp.where(live, g, NEG_INF)
    g_t = g.T
    lf_t = lf.T
    row = lax.broadcasted_iota(jnp.int32, (L, L), 0)
    col = lax.broadcasted_iota(jnp.int32, (L, L), 1)
    lower = row >= col
    b_c = jnp.dot(jnp.where(lower, 1.0, 0.0), lf, precision=lax.Precision.HIGHEST,
                  preferred_element_type=F32)
    b_r = jnp.dot(lf_t, jnp.where(lower, 0.0, 1.0) + jnp.where(row == col, 1.0, 0.0),
                  precision=lax.Precision.HIGHEST, preferred_element_type=F32)

    for h in range(M_HEADS):
        sl = slice(h * M_DIM, (h + 1) * M_DIM)
        b_col = b_c[:, SM_MF + h:SM_MF + h + 1]
        ig_col = g[:, SM_MI + h:SM_MI + h + 1]
        a_row = g_t[SM_MI + h:SM_MI + h + 1, :] - b_r[SM_MF + h:SM_MF + h + 1, :]
        m_prev = m_scr[h, 0:1, 0:1]
        d_mat = jnp.where(lower, b_col + a_row, NEG_INF)
        inter = b_col + m_prev
        m_t = jnp.maximum(inter, jnp.max(d_mat, axis=1, keepdims=True))
        w_intra = jnp.exp(d_mat - m_t)
        w_inter = jnp.exp(inter - m_t)
        qh = q_ref[0, :, sl]
        kf = k_ref[0, :, sl].astype(F32) * M_DIM ** -0.5
        vh = v_ref[0, :, sl]
        s = _dot_nt(qh, kf.astype(BF16)) * w_intra
        c_old = c_scr[h]
        n_old = n_scr[h, 0:1, :]
        num = w_inter * _dot(qh, c_old.astype(BF16)) + _dot(s.astype(BF16), vh)
        den = (w_inter * jnp.sum(qh.astype(F32) * n_old, axis=1, keepdims=True)
               + jnp.sum(s, axis=1, keepdims=True))
        hid = num / jnp.maximum(jnp.abs(den), jnp.exp(-m_t))
        m_end = m_t[L - 1:L, :]
        b_last = b_col[L - 1:L, :]
        w_end = jnp.exp(b_last - b_col + ig_col - m_end)
        decay = jnp.exp(b_last + m_prev - m_end)
        kw = w_end * kf
        c_scr[h] = decay * c_old + _dot(kw.T.astype(BF16), vh)
        n_scr[h] = jnp.broadcast_to(decay * n_old + jnp.sum(kw, axis=0, keepdims=True), (8, M_DIM))
        m_scr[h] = jnp.broadcast_to(m_end, (8, LANES))
        gated = _sigmoid(o_ref[0, :, sl].astype(F32)) * hid
        y_ref[0, :, sl] = (_normalize(gated) * ng_ref[:, sl]).astype(BF16)

    c_out[0] = c_scr[...]
    n_out[0] = n_scr[...]
    m_out[0] = m_scr[...]


def _mlstm(main3, small3, gate_bias, norm_g, c0, n0, m0, *, chunk, valid_len):
    b, t, _ = main3.shape
    blk = lambda col: pl.BlockSpec((1, chunk, M_WIDTH), lambda i, j: (i, j, col // M_WIDTH))
    st = lambda shape: pl.BlockSpec((1,) + shape, lambda i, j: (i,) + (0,) * len(shape))
    c_shape, n_shape, m_shape = (M_HEADS, M_DIM, M_DIM), (M_HEADS, 8, M_DIM), (M_HEADS, 8, LANES)
    return pl.pallas_call(
        functools.partial(_mlstm_kernel, chunk=chunk, valid_len=valid_len),
        grid=(b, t // chunk),
        in_specs=[blk(COL_MQ), blk(COL_MK), blk(COL_MV), blk(COL_MO),
                  pl.BlockSpec((1, chunk, LANES), lambda i, j: (i, j, 0)),
                  pl.BlockSpec((1, LANES), lambda i, j: (0, 0)),
                  pl.BlockSpec((1, M_WIDTH), lambda i, j: (0, 0)),
                  st(c_shape), st(n_shape), st(m_shape)],
        out_specs=[pl.BlockSpec((1, chunk, M_WIDTH), lambda i, j: (i, j, 0)),
                   st(c_shape), st(n_shape), st(m_shape)],
        out_shape=[jax.ShapeDtypeStruct((b, t, M_WIDTH), BF16),
                   jax.ShapeDtypeStruct((b,) + c_shape, F32),
                   jax.ShapeDtypeStruct((b,) + n_shape, F32),
                   jax.ShapeDtypeStruct((b,) + m_shape, F32)],
        scratch_shapes=[pltpu.VMEM(c_shape, F32), pltpu.VMEM(n_shape, F32), pltpu.VMEM(m_shape, F32)],
        compiler_params=_cparams(("arbitrary", "arbitrary")),
        name="mlstm",
    )(main3, main3, main3, main3, small3, gate_bias, norm_g.reshape(1, M_WIDTH), c0, n0, m0)


def _conv_kernel(cu_ref, halo_ref, st_ref, w_ref, cb_ref, ng_ref, nb_ref, y_ref, ns_ref, full_scr,
                 *, tm, valid_len):
    i = pl.program_id(1)

    def glu(u):
        return u[:, :CONV_CH].astype(F32) * _sigmoid(u[:, CONV_CH:].astype(F32))

    @pl.when(i == 0)
    def _():
        full_scr[0:CONV_HALO, :] = st_ref[0]

    @pl.when(i > 0)
    def _():
        full_scr[0:CONV_HALO, :] = glu(halo_ref[0])

    full_scr[CONV_HALO:CONV_HALO + tm, :] = glu(cu_ref[0])
    acc = jnp.zeros((tm, CONV_CH), F32)
    for j in range(CONV_W):
        acc = acc + w_ref[j:j + 1, :] * full_scr[pl.ds(CONV_PAD + j, tm), :]
    y = _normalize(acc + cb_ref[...]) * ng_ref[...] + nb_ref[...]
    y_ref[0] = (y * _sigmoid(y)).astype(BF16)
    ns_ref[0] = full_scr[pl.ds(valid_len, CONV_HALO), :]


def _conv(main3, state, w, bias, ng, nb, *, tm, valid_len):
    b, t, _ = main3.shape
    cu_blk = COL_CU // (2 * CONV_CH)
    hpb = tm // CONV_HALO
    vec = pl.BlockSpec((1, CONV_CH), lambda i, j: (0, 0))
    return pl.pallas_call(
        functools.partial(_conv_kernel, tm=tm, valid_len=valid_len),
        grid=(b, t // tm),
        in_specs=[pl.BlockSpec((1, tm, 2 * CONV_CH), lambda i, j: (i, j, cu_blk)),
                  pl.BlockSpec((1, CONV_HALO, 2 * CONV_CH), lambda i, j: (i, jnp.maximum(j * hpb - 1, 0), cu_blk)),
                  pl.BlockSpec((1, CONV_HALO, CONV_CH), lambda i, j: (i, 0, 0)),
                  pl.BlockSpec((CONV_W, CONV_CH), lambda i, j: (0, 0)), vec, vec, vec],
        out_specs=[pl.BlockSpec((1, tm, CONV_CH), lambda i, j: (i, j, 0)),
                   pl.BlockSpec((1, CONV_HALO, CONV_CH), lambda i, j: (i, 0, 0))],
        out_shape=[jax.ShapeDtypeStruct((b, t, CONV_CH), BF16),
                   jax.ShapeDtypeStruct((b, CONV_HALO, CONV_CH), F32)],
        scratch_shapes=[pltpu.VMEM((CONV_HALO + tm, CONV_CH), F32)],
        compiler_params=_cparams(("arbitrary", "arbitrary")),
        name="conv",
    )(main3, main3, state, w, bias.reshape(1, CONV_CH), ng.reshape(1, CONV_CH), nb.reshape(1, CONV_CH))


def _split_w_in(w_in):
    sizes = (A_WIDTH, KV_WIDTH, KV_WIDTH, IDX_HEADS * IDX_DIM, IDX_DIM, IDX_HEADS,
             M_WIDTH, M_WIDTH, M_WIDTH, M_HEADS, M_HEADS, M_WIDTH, 2 * CONV_CH)
    parts, start = [], 0
    for size in sizes:
        parts.append(w_in[:, start:start + size])
        start += size
    aq, ak, av, iq, ik, iw, mq, mk, mv, mi, mf, mo, cu = parts
    aq = aq * ATTN_LOG2E_SCALE
    w_main = jnp.concatenate([aq, ak, av, iq, mq, mk, mv, mo, cu], axis=1).astype(BF16)
    pad = jnp.zeros((w_in.shape[0], LANES - SM_USED), w_in.dtype)
    w_small = jnp.concatenate([ik, iw, mi, mf, pad], axis=1).astype(BF16)
    return w_main, w_small


def _tiles(m, rows_per_batch, per_row):
    if per_row:
        return dict(inproj=m, outproj=m, ffn=m, dsa_q=128, dsa_k=512, mlstm=128, conv=128)
    t = rows_per_batch
    return dict(inproj=min(1024, t), outproj=min(512, t), ffn=min(1024, t),
                dsa_q=min(128, t), dsa_k=min(512, t), mlstm=min(256, t), conv=min(512, t))


def _layer(x, ada, p, cache, *, rows_per_batch, alpha):
    m, d = x.shape
    nb = m // rows_per_batch
    t = rows_per_batch
    per_row = cache is not None
    tiles = _tiles(m, t, per_row)
    if per_row:
        mods = [jnp.repeat(v, t, axis=0)[None] for v in jnp.split(ada, 6, axis=-1)]
    else:
        mods = [v[:, None, :] for v in jnp.split(ada, 6, axis=-1)]
    sh1, sc1, g1, sh2, sc2, g2 = mods

    main, kv, small = _inproj(x, sh1, sc1, p['w_main'], p['w_small'], per_row, t, tiles['inproj'])
    k_new = kv[:, :KV_WIDTH].reshape(nb, t, A_KV_HEADS, A_HEAD_DIM)
    v_new = kv[:, KV_WIDTH:].reshape(nb, t, A_KV_HEADS, A_HEAD_DIM)
    ik_new = small[:, SM_IK:SM_IK + IDX_DIM].reshape(nb, t, IDX_DIM)

    gate_bias = jnp.zeros((1, LANES), F32)
    gate_bias = gate_bias.at[0, SM_MI:SM_MI + M_HEADS].set(p['b_igate'])
    gate_bias = gate_bias.at[0, SM_MF:SM_MF + M_HEADS].set(p['b_fgate'])

    if cache is None:
        main3 = main.reshape(nb, t, MAIN_COLS)
        small3 = small.reshape(nb, t, LANES)
        ki = ik_new.astype(BF16)
        wi = jnp.swapaxes(small3[:, :, SM_IW:SM_IW + IDX_HEADS], 1, 2)
        attn = _dsa(main3, main3, ki, wi, k_col=COL_K // KV_WIDTH, v_col=COL_V // KV_WIDTH,
                    q_off=0, n_keys=t, tq=tiles['dsa_q'], ts=tiles['dsa_k'])
        c0 = jnp.zeros((nb, M_HEADS, M_DIM, M_DIM), F32)
        n0 = jnp.zeros((nb, M_HEADS, 8, M_DIM), F32)
        m0 = jnp.zeros((nb, M_HEADS, 8, LANES), F32)
        mls, c_new, n_new, m_new = _mlstm(main3, small3, gate_bias, p['mlstm_norm_g'], c0, n0, m0,
                                          chunk=tiles['mlstm'], valid_len=tiles['mlstm'])
        state = jnp.zeros((nb, CONV_HALO, CONV_CH), F32)
        cnv, conv_new = _conv(main3, state, p['conv_w'], p['conv_b'], p['conv_norm_g'], p['conv_norm_b'],
                              tm=tiles['conv'], valid_len=tiles['conv'])
        attn, mls, cnv = (a.reshape(m, -1) for a in (attn, mls, cnv))
    else:
        ck, cv, cik, c_st, n_st, m_st, conv_st = cache
        past = ck.shape[1]
        n_keys = past + t
        tp = tiles['dsa_q']
        ts = tiles['dsa_k']
        s_pad = -(-n_keys // ts) * ts
        main3 = jnp.pad(main.reshape(nb, t, MAIN_COLS), ((0, 0), (0, tp - t), (0, 0)))
        small3 = jnp.pad(small.reshape(nb, t, LANES), ((0, 0), (0, tp - t), (0, 0)))
        pad_keys = lambda a: jnp.pad(a, ((0, 0), (0, s_pad - n_keys), (0, 0)))
        k_all = pad_keys(jnp.concatenate([ck.reshape(nb, past, KV_WIDTH), kv[:, :KV_WIDTH].reshape(nb, t, KV_WIDTH)], axis=1))
        v_all = pad_keys(jnp.concatenate([cv.reshape(nb, past, KV_WIDTH), kv[:, KV_WIDTH:].reshape(nb, t, KV_WIDTH)], axis=1))
        kv_all = jnp.concatenate([k_all, v_all], axis=-1).astype(BF16)
        ki = pad_keys(jnp.concatenate([cik, ik_new], axis=1)).astype(BF16)
        wi = jnp.swapaxes(small3[:, :, SM_IW:SM_IW + IDX_HEADS], 1, 2)
        attn = _dsa(main3, kv_all, ki, wi, k_col=0, v_col=1, q_off=past, n_keys=n_keys, tq=tp, ts=ts)
        n0 = jnp.broadcast_to(n_st[:, :, None, :], (nb, M_HEADS, 8, M_DIM))
        m0 = jnp.broadcast_to(m_st[:, :, None, None], (nb, M_HEADS, 8, LANES))
        mls, c_new, n_new, m_new = _mlstm(main3, small3, gate_bias, p['mlstm_norm_g'], c_st, n0, m0,
                                          chunk=tp, valid_len=t)
        state = jnp.pad(conv_st, ((0, 0), (CONV_PAD, 0), (0, 0)))
        cnv, conv_new = _conv(main3, state, p['conv_w'], p['conv_b'], p['conv_norm_g'], p['conv_norm_b'],
                              tm=tp, valid_len=t)
        attn, mls, cnv = (a[:, :t].reshape(m, -1) for a in (attn, mls, cnv))

    x = _outproj(attn, mls, cnv, x, g1, p['w_out'], p['ln1_g'], p['ln1_b'], per_row, t, tiles['outproj'], alpha)
    x = _ffn(x, sh2, sc2, g2, p['w_ff1'], p['w_ff2'], p['ln2_g'], p['ln2_b'], per_row, t, tiles['ffn'], alpha)
    new_state = (k_new, v_new, ik_new, c_new, n_new[:, :, 0, :], m_new[:, :, 0, 0], conv_new[:, CONV_PAD:, :])
    return x, new_state


def kernel(x_prompt, x_sample, c_prompt, c_sample, cache_attn_k, cache_attn_v, cache_idx_k,
           state_mlstm_C, state_mlstm_n, state_mlstm_m, state_conv,
           w_ada, b_ada, w_in, b_igate, b_fgate, mlstm_norm_g, conv_w, conv_b,
           conv_norm_g, conv_norm_b, w_out, ln1_g, ln1_b, w_ff1, w_ff2, ln2_g, ln2_b):
    depth = w_ada.shape[0]
    alpha = (2 * depth) ** 0.25
    bp, tp, d = x_prompt.shape
    bs, tsmp, _ = x_sample.shape

    c_all = jnp.concatenate([c_prompt, c_sample], axis=0)
    rows = -(-c_all.shape[0] // 8) * 8
    ada_all = _ada(jnp.pad(c_all, ((0, rows - c_all.shape[0]), (0, 0))), w_ada, b_ada)

    xp = x_prompt.reshape(bp * tp, d)
    xs = x_sample.reshape(bs * tsmp, d)
    st_p, st_s = [], []
    for l in range(depth):
        w_main, w_small = _split_w_in(w_in[l])
        p = {'w_main': w_main, 'w_small': w_small, 'b_igate': b_igate[l], 'b_fgate': b_fgate[l],
             'mlstm_norm_g': mlstm_norm_g[l], 'conv_w': conv_w[l], 'conv_b': conv_b[l],
             'conv_norm_g': conv_norm_g[l], 'conv_norm_b': conv_norm_b[l],
             'w_out': w_out[l].astype(BF16), 'ln1_g': ln1_g[l], 'ln1_b': ln1_b[l],
             'w_ff1': w_ff1[l].astype(BF16), 'w_ff2': w_ff2[l].astype(BF16),
             'ln2_g': ln2_g[l], 'ln2_b': ln2_b[l]}
        xp, sp = _layer(xp, ada_all[l, :bp], p, None, rows_per_batch=tp, alpha=alpha)
        cache = (cache_attn_k[l], cache_attn_v[l], cache_idx_k[l], state_mlstm_C[l],
                 state_mlstm_n[l], state_mlstm_m[l], state_conv[l])
        xs, ss = _layer(xs, ada_all[l, bp:bp + bs], p, cache, rows_per_batch=tsmp, alpha=alpha)
        st_p.append(sp)
        st_s.append(ss)

    def stack(sts, i):
        return jnp.stack([s[i] for s in sts], axis=0)

    return (xp.reshape(bp, tp, d), xs.reshape(bs, tsmp, d),
            *(stack(st_p, i) for i in range(7)), *(stack(st_s, i) for i in range(7)))
```

```python
import functools

import jax
import jax.numpy as jnp
from jax import lax
from jax.experimental import pallas as pl
from jax.experimental.pallas import tpu as pltpu

F32 = jnp.float32
BF16 = jnp.bfloat16

CHUNK = 64
A_HEADS = 8
A_KV_HEADS = 2
A_GROUP = A_HEADS // A_KV_HEADS
A_HEAD_DIM = 128
A_WIDTH = A_HEADS * A_HEAD_DIM
KV_WIDTH = A_KV_HEADS * A_HEAD_DIM
IDX_HEADS = 8
IDX_DIM = 64
TOPK_MAX = 256
M_HEADS = 4
M_DIM = 128
M_WIDTH = M_HEADS * M_DIM
CONV_CH = 512
CONV_W = 31
LN_EPS = 1e-5
NEG_INF = -1e30
ATTN_LOG2E_SCALE = A_HEAD_DIM ** -0.5 * 1.4426950408889634
INT_MIN = -2 ** 31

LANES = 128
VMEM_LIMIT_BYTES = 56 * 1024 * 1024

COL_Q = 0
COL_K = COL_Q + A_WIDTH
COL_V = COL_K + KV_WIDTH
COL_IQ = COL_V + KV_WIDTH
COL_MQ = COL_IQ + IDX_HEADS * IDX_DIM
COL_MK = COL_MQ + M_WIDTH
COL_MV = COL_MK + M_WIDTH
COL_MO = COL_MV + M_WIDTH
COL_CU = COL_MO + M_WIDTH
MAIN_COLS = COL_CU + 2 * CONV_CH
SM_IK = 0
SM_IW = SM_IK + IDX_DIM
SM_MI = SM_IW + IDX_HEADS
SM_MF = SM_MI + M_HEADS
SM_USED = SM_MF + M_HEADS

BISECT_ALWAYS = 16
BISECT_GROUP = 4

CONV_HALO = 32
CONV_PAD = CONV_HALO - (CONV_W - 1)
CONV_ROWS = 32


def _cparams(sem):
    return pltpu.CompilerParams(dimension_semantics=sem, vmem_limit_bytes=VMEM_LIMIT_BYTES)


def _normalize(x):
    mu = jnp.mean(x, axis=-1, keepdims=True)
    xc = x - mu
    var = jnp.mean(xc * xc, axis=-1, keepdims=True)
    return xc * lax.rsqrt(var + LN_EPS)


def _sigmoid(x):
    return 1.0 / (1.0 + jnp.exp(-x))


def _dot(a, b):
    return jnp.dot(a, b, preferred_element_type=F32)


def _dot_nt(a, b):
    return lax.dot_general(a, b, (((1,), (1,)), ((), ())), preferred_element_type=F32)


def _ada_kernel(c_ref, w_ref, b_ref, o_ref):
    c = c_ref[...]
    s = (c * _sigmoid(c)).astype(BF16)
    o_ref[0] = _dot(s, w_ref[0].astype(BF16)) + b_ref[0]


def _ada(c_all, w_ada, b_ada):
    depth, d, n = w_ada.shape
    rows = c_all.shape[0]
    tn = 1024
    return pl.pallas_call(
        _ada_kernel,
        grid=(depth, n // tn),
        in_specs=[pl.BlockSpec((rows, d), lambda l, j: (0, 0)),
                  pl.BlockSpec((1, d, tn), lambda l, j: (l, 0, j)),
                  pl.BlockSpec((1, 1, tn), lambda l, j: (l, 0, j))],
        out_specs=pl.BlockSpec((1, rows, tn), lambda l, j: (l, 0, j)),
        out_shape=jax.ShapeDtypeStruct((depth, rows, n), F32),
        compiler_params=_cparams(("arbitrary", "arbitrary")),
        name="ada",
    )(c_all, w_ada, b_ada.reshape(depth, 1, n))


def _mod_spec(per_row, tm, rows_per_batch, d):
    if per_row:
        return pl.BlockSpec((1, tm, d), lambda i, *_: (0, i, 0))
    bpb = rows_per_batch // tm
    return pl.BlockSpec((1, 1, d), lambda i, *_: (i // bpb, 0, 0))


def _inproj_kernel(x_ref, sh_ref, sc_ref, wm_ref, ws_ref, main_ref, kv_ref, small_ref, h_scr, *, kv_block):
    j = pl.program_id(1)

    @pl.when(j == 0)
    def _():
        h = _normalize(x_ref[...]) * (1.0 + sc_ref[0]) + sh_ref[0]
        hb = h.astype(BF16)
        h_scr[...] = hb
        small_ref[...] = _dot(hb, ws_ref[...])

    acc = _dot(h_scr[...], wm_ref[...])
    main_ref[...] = acc.astype(BF16)

    @pl.when(j == kv_block)
    def _():
        kv_ref[...] = acc


def _inproj(x, sh, sc, w_main, w_small, per_row, rows_per_batch, tm):
    m, d = x.shape
    tn = 2 * KV_WIDTH
    mod = _mod_spec(per_row, tm, rows_per_batch, d)
    return pl.pallas_call(
        functools.partial(_inproj_kernel, kv_block=COL_K // tn),
        grid=(m // tm, MAIN_COLS // tn),
        in_specs=[pl.BlockSpec((tm, d), lambda i, j: (i, 0)), mod, mod,
                  pl.BlockSpec((d, tn), lambda i, j: (0, j)),
                  pl.BlockSpec((d, LANES), lambda i, j: (0, 0))],
        out_specs=[pl.BlockSpec((tm, tn), lambda i, j: (i, j)),
                   pl.BlockSpec((tm, tn), lambda i, j: (i, 0)),
                   pl.BlockSpec((tm, LANES), lambda i, j: (i, 0))],
        out_shape=[jax.ShapeDtypeStruct((m, MAIN_COLS), BF16),
                   jax.ShapeDtypeStruct((m, tn), F32),
                   jax.ShapeDtypeStruct((m, LANES), F32)],
        scratch_shapes=[pltpu.VMEM((tm, d), BF16)],
        compiler_params=_cparams(("arbitrary", "arbitrary")),
        name="inproj",
    )(x, sh, sc, w_main, w_small)


def _outproj_kernel(a_ref, m_ref, c_ref, x_ref, g_ref, w_ref, lg_ref, lb_ref, o_ref, *, alpha):
    y = _dot(a_ref[...], w_ref[0:A_WIDTH, :])
    y += _dot(m_ref[...], w_ref[A_WIDTH:A_WIDTH + M_WIDTH, :])
    y += _dot(c_ref[...], w_ref[A_WIDTH + M_WIDTH:, :])
    z = alpha * x_ref[...] + (1.0 + g_ref[0]) * y
    o_ref[...] = _normalize(z) * lg_ref[...] + lb_ref[...]


def _outproj(attn, mls, cnv, x, gate, w_out, ln_g, ln_b, per_row, rows_per_batch, tm, alpha):
    m, d = x.shape
    row = lambda w: pl.BlockSpec((tm, w), lambda i: (i, 0))
    vec = pl.BlockSpec((1, d), lambda i: (0, 0))
    return pl.pallas_call(
        functools.partial(_outproj_kernel, alpha=alpha),
        grid=(m // tm,),
        in_specs=[row(A_WIDTH), row(M_WIDTH), row(CONV_CH), row(d),
                  _mod_spec(per_row, tm, rows_per_batch, d),
                  pl.BlockSpec(w_out.shape, lambda i: (0, 0)), vec, vec],
        out_specs=row(d),
        out_shape=jax.ShapeDtypeStruct((m, d), F32),
        compiler_params=_cparams(("arbitrary",)),
        name="outproj",
    )(attn, mls, cnv, x, gate, w_out, ln_g.reshape(1, d), ln_b.reshape(1, d))


def _ffn_kernel(x_ref, sh_ref, sc_ref, g_ref, w1_ref, w2_ref, lg_ref, lb_ref, o_ref, h_scr, *, alpha):
    j = pl.program_id(1)

    @pl.when(j == 0)
    def _():
        h = _normalize(x_ref[...]) * (1.0 + sc_ref[0]) + sh_ref[0]
        h_scr[...] = h.astype(BF16)
        o_ref[...] = jnp.zeros_like(o_ref)

    u = jnp.maximum(_dot(h_scr[...], w1_ref[...]), 0.0)
    o_ref[...] += _dot((u * u).astype(BF16), w2_ref[...])

    @pl.when(j == pl.num_programs(1) - 1)
    def _():
        z = alpha * x_ref[...] + (1.0 + g_ref[0]) * o_ref[...]
        o_ref[...] = _normalize(z) * lg_ref[...] + lb_ref[...]


def _ffn(x, sh, sc, gate, w1, w2, ln_g, ln_b, per_row, rows_per_batch, tm, alpha):
    m, d = x.shape
    dff = w1.shape[1]
    tf = 512
    mod = _mod_spec(per_row, tm, rows_per_batch, d)
    vec = pl.BlockSpec((1, d), lambda i, j: (0, 0))
    return pl.pallas_call(
        functools.partial(_ffn_kernel, alpha=alpha),
        grid=(m // tm, dff // tf),
        in_specs=[pl.BlockSpec((tm, d), lambda i, j: (i, 0)), mod, mod, mod,
                  pl.BlockSpec((d, tf), lambda i, j: (0, j)),
                  pl.BlockSpec((tf, d), lambda i, j: (j, 0)), vec, vec],
        out_specs=pl.BlockSpec((tm, d), lambda i, j: (i, 0)),
        out_shape=jax.ShapeDtypeStruct((m, d), F32),
        scratch_shapes=[pltpu.VMEM((tm, d), BF16)],
        compiler_params=_cparams(("arbitrary", "arbitrary")),
        name="ffn",
    )(x, sh, sc, gate, w1, w2, ln_g.reshape(1, d), ln_b.reshape(1, d))


def _dsa_kernel(q_ref, qi_ref, wi_ref, k_ref, v_ref, ki_ref, o_ref,
                key_scr, thr_scr, qs_scr, qis_scr, s_scr, m_scr, l_scr, acc_scr,
                *, tq, ts, q_off, n_keys, topk, n_tiles_static):
    qb = pl.program_id(1)
    q_pos0 = q_off + qb * tq
    if n_tiles_static is None:
        vis_max = jnp.minimum(((q_pos0 + tq - 1) // CHUNK + 1) * CHUNK, n_keys)
        n_tiles = (vis_max + ts - 1) // ts
    else:
        n_tiles = n_tiles_static
    lane_tiles = [slice(c * LANES, (c + 1) * LANES) for c in range(ts // LANES)]
    sub_tiles = [slice(r * 8, (r + 1) * 8) for r in range(ts // 8)]

    q_chunk = (q_pos0 + lax.broadcasted_iota(jnp.int32, (1, tq), 1)) // CHUNK
    n_vis = jnp.minimum((q_chunk + 1) * CHUNK, n_keys)
    kk = jnp.minimum(topk, n_vis).astype(F32)

    eye = (lax.broadcasted_iota(jnp.int32, (tq, tq), 0)
           == lax.broadcasted_iota(jnp.int32, (tq, tq), 1)).astype(BF16)
    for h in range(A_HEADS):
        rows = slice((h % A_GROUP) * tq, (h % A_GROUP + 1) * tq)
        qs_scr[h // A_GROUP, rows, 0:A_HEAD_DIM] = q_ref[0, :, h * A_HEAD_DIM:(h + 1) * A_HEAD_DIM]
        qs_scr[h // A_GROUP, rows, A_HEAD_DIM:A_HEAD_DIM + tq] = eye
    for h in range(IDX_HEADS):
        qis_scr[h // 2, (h % 2) * tq:(h % 2 + 1) * tq, :] = qi_ref[0, :, h * IDX_DIM:(h + 1) * IDX_DIM]
    wi = wi_ref[0] * IDX_HEADS ** -0.5 * IDX_DIM ** -0.5

    def score_tile(t):
        start = pl.multiple_of(t * ts, ts)
        kt = ki_ref[0, pl.ds(start, ts), :]
        score = None
        for pair in range(IDX_HEADS // 2):
            d = _dot_nt(kt, qis_scr[pair])
            for e in range(2):
                h = 2 * pair + e
                term = wi[h:h + 1, :] * jnp.maximum(d[:, e * tq:(e + 1) * tq], 0.0)
                score = term if score is None else score + term
        score = jnp.where(score == 0.0, 0.0, score)
        bits = lax.bitcast_convert_type(score, jnp.int32)
        key = bits ^ ((bits >> 31) & 0x7FFFFFFF)
        visible = lax.broadcasted_iota(jnp.int32, (ts, tq), 0) < n_vis - start
        key_scr[t] = jnp.where(visible, key, INT_MIN)

    def score_pair(i, carry):
        score_tile(2 * i)
        score_tile(2 * i + 1)
        return carry

    lax.fori_loop(0, n_tiles // 2, score_pair, 0)

    @pl.when(n_tiles % 2 == 1)
    def _():
        score_tile(n_tiles - 1)

    n_acc = 8

    def count(pred):
        def body(t, accs):
            accs = list(accs)
            for r, ss in enumerate(sub_tiles):
                accs[r % n_acc] = accs[r % n_acc] + jnp.where(pred(key_scr[t, ss, :]), 1.0, 0.0)
            return tuple(accs)
        accs = lax.fori_loop(0, n_tiles, body, tuple(jnp.zeros((8, tq), F32) for _ in range(n_acc)))
        total = accs[0]
        for a in accs[1:]:
            total = total + a
        return jnp.sum(total, axis=0, keepdims=True)

    thr_scr[0:1, :] = jnp.zeros((1, tq), jnp.int32)
    thr_scr[1:2, :] = lax.bitcast_convert_type(jnp.full((1, tq), n_tiles * ts, F32), jnp.int32)

    def bisect(i, carry):
        t_u = thr_scr[0:1, :]
        trial = t_u | lax.shift_left(jnp.int32(1), 31 - i)
        cand = jnp.broadcast_to(trial ^ INT_MIN, (8, tq))
        c = count(lambda blk: blk >= cand)
        take = c >= kk
        thr_scr[0:1, :] = jnp.where(take, trial, t_u)
        thr_scr[1:2, :] = jnp.where(take, lax.bitcast_convert_type(c, jnp.int32), thr_scr[1:2, :])
        return carry

    def n_ge_now():
        return lax.bitcast_convert_type(thr_scr[1:2, :], F32)

    lax.fori_loop(0, BISECT_ALWAYS, bisect, 0)
    for first in range(BISECT_ALWAYS, 32, BISECT_GROUP):
        @pl.when(jnp.max(n_ge_now() - kk) > 0.0)
        def _():
            lax.fori_loop(first, first + BISECT_GROUP, bisect, 0)

    thr = jnp.broadcast_to(thr_scr[0:1, :] ^ INT_MIN, (8, tq))
    n_ge = n_ge_now()

    @pl.when(jnp.max(n_ge - kk) > 0.0)
    def _():
        n_gt = count(lambda blk: blk > thr)
        keep = kk - n_gt
        lower = (lax.broadcasted_iota(jnp.int32, (ts, ts), 0)
                 >= lax.broadcasted_iota(jnp.int32, (ts, ts), 1)).astype(BF16)

        def tie_tile(t, seen):
            blk = key_scr[t]
            eq = blk == thr[0:1, :]
            rank = seen + _dot(lower, jnp.where(eq, 1.0, 0.0).astype(BF16))
            key_scr[t] = jnp.where(eq & (rank > keep), INT_MIN, blk)
            return rank[ts - 1:ts, :]

        lax.fori_loop(0, n_tiles, tie_tile, jnp.zeros((1, tq), F32))

    m_scr[...] = jnp.full(m_scr.shape, NEG_INF, F32)
    l_scr[...] = jnp.zeros(l_scr.shape, F32)
    acc_scr[...] = jnp.zeros(acc_scr.shape, F32)

    def logits_tile(t, slot):
        start = pl.multiple_of(t * ts, ts)
        bias = jnp.where(key_scr[t] >= thr[0:1, :], 0.0, NEG_INF).astype(BF16)
        for n in range(A_KV_HEADS):
            kt = k_ref[0, pl.ds(start, ts), n * A_HEAD_DIM:(n + 1) * A_HEAD_DIM]
            s_scr[slot, n] = _dot_nt(qs_scr[n], jnp.concatenate([kt, bias], axis=1))

    def softmax_pv_tile(t, slot):
        start = pl.multiple_of(t * ts, ts)
        for n in range(A_KV_HEADS):
            vt = v_ref[0, pl.ds(start, ts), n * A_HEAD_DIM:(n + 1) * A_HEAD_DIM]
            for g in range(A_GROUP):
                h = n * A_GROUP + g
                s = [s_scr[slot, n, g * tq:(g + 1) * tq, ls] for ls in lane_tiles]
                mx = s[0]
                for sc in s[1:]:
                    mx = jnp.maximum(mx, sc)
                m_old = m_scr[h]
                m_new = jnp.maximum(m_old, jnp.max(mx, axis=1, keepdims=True))
                alpha = jnp.exp2(m_old - m_new)
                p = [jnp.exp2(sc - m_new) for sc in s]
                psum = p[0]
                for pc in p[1:]:
                    psum = psum + pc
                l_scr[h] = alpha * l_scr[h] + psum
                pv = _dot(jnp.concatenate(p, axis=1).astype(BF16), vt)
                acc_scr[h] = alpha * acc_scr[h] + pv
                m_scr[h] = m_new

    def attend_pair(i, carry):
        t = 2 * i
        logits_tile(t + 1, 1)
        softmax_pv_tile(t, 0)
        logits_tile(t + 2, 0)
        softmax_pv_tile(t + 1, 1)
        return carry

    n_pairs = (n_tiles - 1) // 2
    logits_tile(0, 0)
    lax.fori_loop(0, n_pairs, attend_pair, 0)
    t_rest = 2 * n_pairs
    softmax_pv_tile(t_rest, 0)

    @pl.when(t_rest + 1 < n_tiles)
    def _():
        logits_tile(t_rest + 1, 1)
        softmax_pv_tile(t_rest + 1, 1)

    for h in range(A_HEADS):
        l_tot = jnp.sum(l_scr[h], axis=1, keepdims=True)
        o_ref[0, :, h * A_HEAD_DIM:(h + 1) * A_HEAD_DIM] = (acc_scr[h] / l_tot).astype(BF16)


def _dsa(main3, keys_src, ki, wi, *, k_col, v_col, q_off, n_keys, tq, ts):
    assert tq == A_HEAD_DIM, "the mask rides in the unused half of a 2*A_HEAD_DIM-deep contraction"
    b, t_q, _ = main3.shape
    s_len = keys_src.shape[1]
    topk = min(TOPK_MAX, n_keys // 4)
    dynamic = q_off == 0
    kern = functools.partial(_dsa_kernel, tq=tq, ts=ts, q_off=q_off, n_keys=n_keys, topk=topk,
                             n_tiles_static=None if dynamic else s_len // ts)
    return pl.pallas_call(
        kern,
        grid=(b, t_q // tq),
        in_specs=[pl.BlockSpec((1, tq, A_WIDTH), lambda i, j: (i, j, COL_Q // A_WIDTH)),
                  pl.BlockSpec((1, tq, IDX_HEADS * IDX_DIM), lambda i, j: (i, j, COL_IQ // (IDX_HEADS * IDX_DIM))),
                  pl.BlockSpec((1, IDX_HEADS, tq), lambda i, j: (i, 0, j)),
                  pl.BlockSpec((1, s_len, KV_WIDTH), lambda i, j: (i, 0, k_col)),
                  pl.BlockSpec((1, s_len, KV_WIDTH), lambda i, j: (i, 0, v_col)),
                  pl.BlockSpec((1, s_len, IDX_DIM), lambda i, j: (i, 0, 0))],
        out_specs=pl.BlockSpec((1, tq, A_WIDTH), lambda i, j: (i, j, 0)),
        out_shape=jax.ShapeDtypeStruct((b, t_q, A_WIDTH), BF16),
        scratch_shapes=[pltpu.VMEM((s_len // ts, ts, tq), jnp.int32),
                        pltpu.VMEM((8, tq), jnp.int32),
                        pltpu.VMEM((A_KV_HEADS, A_GROUP * tq, A_HEAD_DIM + tq), BF16),
                        pltpu.VMEM((IDX_HEADS // 2, 2 * tq, IDX_DIM), BF16),
                        pltpu.VMEM((2, A_KV_HEADS, A_GROUP * tq, ts), F32),
                        pltpu.VMEM((A_HEADS, tq, LANES), F32),
                        pltpu.VMEM((A_HEADS, tq, LANES), F32),
                        pltpu.VMEM((A_HEADS, tq, A_HEAD_DIM), F32)],
        compiler_params=_cparams(("arbitrary", "arbitrary")),
        name="dsa",
    )(main3, main3, wi, keys_src, keys_src, ki)


def _log_sigmoid(x):
    return jnp.minimum(x, 0.0) - jnp.log1p(jnp.exp(-jnp.abs(x)))


def _mlstm_kernel(q_ref, k_ref, v_ref, o_ref, g_ref, gb_ref, ng_ref, c0_ref, n0_ref, m0_ref,
                  y_ref, c_out, n_out, m_out, c_scr, n_scr, m_scr, *, chunk, valid_len, batches):
    step = pl.program_id(1)

    @pl.when(step == 0)
    def _():
        c_scr[...] = c0_ref[...]
        n_scr[...] = n0_ref[...]
        m_scr[...] = m0_ref[...]

    for bi in range(batches):
        _mlstm_chunk(bi, q_ref, k_ref, v_ref, o_ref, g_ref, gb_ref, ng_ref, y_ref, c_scr, n_scr, m_scr,
                     chunk=chunk, valid_len=valid_len)

    c_out[...] = c_scr[...]
    n_out[...] = n_scr[...]
    m_out[...] = m_scr[...]


def _mlstm_chunk(bi, q_ref, k_ref, v_ref, o_ref, g_ref, gb_ref, ng_ref, y_ref, c_scr, n_scr, m_scr,
                 *, chunk, valid_len):
    L = chunk
    g = g_ref[bi] + gb_ref[...]
    lf = _log_sigmoid(g)
    if valid_len < L:
        live = lax.broadcasted_iota(jnp.int32, (L, 1), 0) < valid_len
        lf = jnp.where(live, lf, 0.0)
        g = jnp.where(live, g, NEG_INF)
    g_t = g.T
    lf_t = lf.T
    row = lax.broadcasted_iota(jnp.int32, (L, L), 0)
    col = lax.broadcasted_iota(jnp.int32, (L, L), 1)
    lower = row >= col
    b_c = jnp.dot(jnp.where(lower, 1.0, 0.0), lf, precision=lax.Precision.HIGHEST,
                  preferred_element_type=F32)
    b_r = jnp.dot(lf_t, jnp.where(lower, 0.0, 1.0) + jnp.where(row == col, 1.0, 0.0),
                  precision=lax.Precision.HIGHEST, preferred_element_type=F32)

    for h in range(M_HEADS):
        sl = slice(h * M_DIM, (h + 1) * M_DIM)
        b_col = b_c[:, SM_MF + h:SM_MF + h + 1]
        ig_col = g[:, SM_MI + h:SM_MI + h + 1]
        a_row = g_t[SM_MI + h:SM_MI + h + 1, :] - b_r[SM_MF + h:SM_MF + h + 1, :]
        m_prev = m_scr[bi, h, 0:1, 0:1]
        d_mat = jnp.where(lower, b_col + a_row, NEG_INF)
        inter = b_col + m_prev
        m_t = jnp.maximum(inter, jnp.max(d_mat, axis=1, keepdims=True))
        w_intra = jnp.exp(d_mat - m_t)
        w_inter = jnp.exp(inter - m_t)
        qh = q_ref[bi, :, sl]
        kf = k_ref[bi, :, sl].astype(F32) * M_DIM ** -0.5
        vh = v_ref[bi, :, sl]
        s = _dot_nt(qh, kf.astype(BF16)) * w_intra
        c_old = c_scr[bi, h]
        n_old = n_scr[bi, h, 0:1, :]
        num = w_inter * _dot(qh, c_old.astype(BF16)) + _dot(s.astype(BF16), vh)
        den = (w_inter * jnp.sum(qh.astype(F32) * n_old, axis=1, keepdims=True)
               + jnp.sum(s, axis=1, keepdims=True))
        hid = num / jnp.maximum(jnp.abs(den), jnp.exp(-m_t))
        m_end = m_t[L - 1:L, :]
        b_last = b_col[L - 1:L, :]
        w_end = jnp.exp(b_last - b_col + ig_col - m_end)
        decay = jnp.exp(b_last + m_prev - m_end)
        kw = w_end * kf
        c_scr[bi, h] = decay * c_old + _dot(kw.T.astype(BF16), vh)
        n_scr[bi, h] = jnp.broadcast_to(decay * n_old + jnp.sum(kw, axis=0, keepdims=True), (8, M_DIM))
        m_scr[bi, h] = jnp.broadcast_to(m_end, (8, LANES))
        gated = _sigmoid(o_ref[bi, :, sl].astype(F32)) * hid
        y_ref[bi, :, sl] = (_normalize(gated) * ng_ref[:, sl]).astype(BF16)


def _mlstm(main3, small3, gate_bias, norm_g, c0, n0, m0, *, chunk, valid_len, batches):
    b, t, _ = main3.shape
    bb = batches
    blk = lambda col: pl.BlockSpec((bb, chunk, M_WIDTH), lambda i, j: (i, j, col // M_WIDTH))
    st = lambda shape: pl.BlockSpec((bb,) + shape, lambda i, j: (i,) + (0,) * len(shape))
    c_shape, n_shape, m_shape = (M_HEADS, M_DIM, M_DIM), (M_HEADS, 8, M_DIM), (M_HEADS, 8, LANES)
    return pl.pallas_call(
        functools.partial(_mlstm_kernel, chunk=chunk, valid_len=valid_len, batches=bb),
        grid=(b // bb, t // chunk),
        in_specs=[blk(COL_MQ), blk(COL_MK), blk(COL_MV), blk(COL_MO),
                  pl.BlockSpec((bb, chunk, LANES), lambda i, j: (i, j, 0)),
                  pl.BlockSpec((1, LANES), lambda i, j: (0, 0)),
                  pl.BlockSpec((1, M_WIDTH), lambda i, j: (0, 0)),
                  st(c_shape), st(n_shape), st(m_shape)],
        out_specs=[pl.BlockSpec((bb, chunk, M_WIDTH), lambda i, j: (i, j, 0)),
                   st(c_shape), st(n_shape), st(m_shape)],
        out_shape=[jax.ShapeDtypeStruct((b, t, M_WIDTH), BF16),
                   jax.ShapeDtypeStruct((b,) + c_shape, F32),
                   jax.ShapeDtypeStruct((b,) + n_shape, F32),
                   jax.ShapeDtypeStruct((b,) + m_shape, F32)],
        scratch_shapes=[pltpu.VMEM((bb,) + c_shape, F32), pltpu.VMEM((bb,) + n_shape, F32),
                        pltpu.VMEM((bb,) + m_shape, F32)],
        compiler_params=_cparams(("arbitrary", "arbitrary")),
        name="mlstm",
    )(main3, main3, main3, main3, small3, gate_bias, norm_g.reshape(1, M_WIDTH), c0, n0, m0)


def _conv_kernel(cu_ref, halo_ref, st_ref, w_ref, cb_ref, ng_ref, nb_ref, y_ref, ns_ref, full_scr, sh_scr,
                 *, tm, valid_len):
    i = pl.program_id(1)

    def glu(u):
        return u[:, :CONV_CH].astype(F32) * _sigmoid(u[:, CONV_CH:].astype(F32))

    @pl.when(i == 0)
    def _():
        full_scr[0:CONV_HALO, :] = st_ref[0]

    @pl.when(i > 0)
    def _():
        full_scr[0:CONV_HALO, :] = glu(halo_ref[0])

    full_scr[CONV_HALO:CONV_HALO + tm, :] = glu(cu_ref[0])
    span = tm + CONV_HALO - 8
    for b in range(1, 8):
        sh_scr[b - 1] = full_scr[pl.ds(b, span), :]
    for c0 in range(0, tm, CONV_ROWS):
        acc = None
        for j in range(CONV_W):
            a8, b = (CONV_PAD + j) // 8 * 8, (CONV_PAD + j) % 8
            if b == 0:
                src = full_scr[pl.ds(c0 + a8, CONV_ROWS), :]
            else:
                src = sh_scr[b - 1, pl.ds(c0 + a8, CONV_ROWS), :]
            term = w_ref[j:j + 1, :] * src
            acc = term if acc is None else acc + term
        y = _normalize(acc + cb_ref[...]) * ng_ref[...] + nb_ref[...]
        y_ref[0, c0:c0 + CONV_ROWS, :] = (y * _sigmoid(y)).astype(BF16)
    ns_ref[0] = full_scr[pl.ds(valid_len, CONV_HALO), :]


def _conv(main3, state, w, bias, ng, nb, *, tm, valid_len):
    b, t, _ = main3.shape
    cu_blk = COL_CU // (2 * CONV_CH)
    hpb = tm // CONV_HALO
    vec = pl.BlockSpec((1, CONV_CH), lambda i, j: (0, 0))
    return pl.pallas_call(
        functools.partial(_conv_kernel, tm=tm, valid_len=valid_len),
        grid=(b, t // tm),
        in_specs=[pl.BlockSpec((1, tm, 2 * CONV_CH), lambda i, j: (i, j, cu_blk)),
                  pl.BlockSpec((1, CONV_HALO, 2 * CONV_CH), lambda i, j: (i, jnp.maximum(j * hpb - 1, 0), cu_blk)),
                  pl.BlockSpec((1, CONV_HALO, CONV_CH), lambda i, j: (i, 0, 0)),
                  pl.BlockSpec((CONV_W, CONV_CH), lambda i, j: (0, 0)), vec, vec, vec],
        out_specs=[pl.BlockSpec((1, tm, CONV_CH), lambda i, j: (i, j, 0)),
                   pl.BlockSpec((1, CONV_HALO, CONV_CH), lambda i, j: (i, 0, 0))],
        out_shape=[jax.ShapeDtypeStruct((b, t, CONV_CH), BF16),
                   jax.ShapeDtypeStruct((b, CONV_HALO, CONV_CH), F32)],
        scratch_shapes=[pltpu.VMEM((CONV_HALO + tm, CONV_CH), F32),
                        pltpu.VMEM((7, CONV_HALO + tm - 8, CONV_CH), F32)],
        compiler_params=_cparams(("arbitrary", "arbitrary")),
        name="conv",
    )(main3, main3, state, w, bias.reshape(1, CONV_CH), ng.reshape(1, CONV_CH), nb.reshape(1, CONV_CH))


def _split_w_in(w_in):
    sizes = (A_WIDTH, KV_WIDTH, KV_WIDTH, IDX_HEADS * IDX_DIM, IDX_DIM, IDX_HEADS,
             M_WIDTH, M_WIDTH, M_WIDTH, M_HEADS, M_HEADS, M_WIDTH, 2 * CONV_CH)
    parts, start = [], 0
    for size in sizes:
        parts.append(w_in[:, start:start + size])
        start += size
    aq, ak, av, iq, ik, iw, mq, mk, mv, mi, mf, mo, cu = parts
    aq = aq * ATTN_LOG2E_SCALE
    w_main = jnp.concatenate([aq, ak, av, iq, mq, mk, mv, mo, cu], axis=1).astype(BF16)
    pad = jnp.zeros((w_in.shape[0], LANES - SM_USED), w_in.dtype)
    w_small = jnp.concatenate([ik, iw, mi, mf, pad], axis=1).astype(BF16)
    return w_main, w_small


def _tiles(m, rows_per_batch, per_row):
    if per_row:
        return dict(inproj=m, outproj=m, ffn=m, dsa_q=128, dsa_k=512, mlstm=128, mlstm_batches=1, conv=128)
    t = rows_per_batch
    return dict(inproj=min(1024, t), outproj=min(512, t), ffn=min(1024, t),
                dsa_q=min(128, t), dsa_k=min(512, t), mlstm=min(256, t), mlstm_batches=1, conv=min(512, t))


def _layer(x, ada, p, cache, *, rows_per_batch, alpha):
    m, d = x.shape
    nb = m // rows_per_batch
    t = rows_per_batch
    per_row = cache is not None
    tiles = _tiles(m, t, per_row)
    if per_row:
        mods = [jnp.repeat(v, t, axis=0)[None] for v in jnp.split(ada, 6, axis=-1)]
    else:
        mods = [v[:, None, :] for v in jnp.split(ada, 6, axis=-1)]
    sh1, sc1, g1, sh2, sc2, g2 = mods

    main, kv, small = _inproj(x, sh1, sc1, p['w_main'], p['w_small'], per_row, t, tiles['inproj'])
    k_new = kv[:, :KV_WIDTH].reshape(nb, t, A_KV_HEADS, A_HEAD_DIM)
    v_new = kv[:, KV_WIDTH:].reshape(nb, t, A_KV_HEADS, A_HEAD_DIM)
    ik_new = small[:, SM_IK:SM_IK + IDX_DIM].reshape(nb, t, IDX_DIM)

    gate_bias = jnp.zeros((1, LANES), F32)
    gate_bias = gate_bias.at[0, SM_MI:SM_MI + M_HEADS].set(p['b_igate'])
    gate_bias = gate_bias.at[0, SM_MF:SM_MF + M_HEADS].set(p['b_fgate'])

    if cache is None:
        main3 = main.reshape(nb, t, MAIN_COLS)
        small3 = small.reshape(nb, t, LANES)
        ki = ik_new.astype(BF16)
        wi = jnp.swapaxes(small3[:, :, SM_IW:SM_IW + IDX_HEADS], 1, 2)
        attn = _dsa(main3, main3, ki, wi, k_col=COL_K // KV_WIDTH, v_col=COL_V // KV_WIDTH,
                    q_off=0, n_keys=t, tq=tiles['dsa_q'], ts=tiles['dsa_k'])
        c0 = jnp.zeros((nb, M_HEADS, M_DIM, M_DIM), F32)
        n0 = jnp.zeros((nb, M_HEADS, 8, M_DIM), F32)
        m0 = jnp.zeros((nb, M_HEADS, 8, LANES), F32)
        mls, c_new, n_new, m_new = _mlstm(main3, small3, gate_bias, p['mlstm_norm_g'], c0, n0, m0,
                                          chunk=tiles['mlstm'], valid_len=tiles['mlstm'],
                                          batches=tiles['mlstm_batches'])
        state = jnp.zeros((nb, CONV_HALO, CONV_CH), F32)
        cnv, conv_new = _conv(main3, state, p['conv_w'], p['conv_b'], p['conv_norm_g'], p['conv_norm_b'],
                              tm=tiles['conv'], valid_len=tiles['conv'])
        attn, mls, cnv = (a.reshape(m, -1) for a in (attn, mls, cnv))
    else:
        ck, cv, cik, c_st, n_st, m_st, conv_st = cache
        past = ck.shape[1]
        n_keys = past + t
        tp = tiles['dsa_q']
        ts = tiles['dsa_k']
        s_pad = -(-n_keys // ts) * ts
        main3 = jnp.pad(main.reshape(nb, t, MAIN_COLS), ((0, 0), (0, tp - t), (0, 0)))
        small3 = jnp.pad(small.reshape(nb, t, LANES), ((0, 0), (0, tp - t), (0, 0)))
        pad_keys = lambda a: jnp.pad(a, ((0, 0), (0, s_pad - n_keys), (0, 0)))
        k_all = pad_keys(jnp.concatenate([ck.reshape(nb, past, KV_WIDTH), kv[:, :KV_WIDTH].reshape(nb, t, KV_WIDTH)], axis=1))
        v_all = pad_keys(jnp.concatenate([cv.reshape(nb, past, KV_WIDTH), kv[:, KV_WIDTH:].reshape(nb, t, KV_WIDTH)], axis=1))
        kv_all = jnp.concatenate([k_all, v_all], axis=-1).astype(BF16)
        ki = pad_keys(jnp.concatenate([cik, ik_new], axis=1)).astype(BF16)
        wi = jnp.swapaxes(small3[:, :, SM_IW:SM_IW + IDX_HEADS], 1, 2)
        attn = _dsa(main3, kv_all, ki, wi, k_col=0, v_col=1, q_off=past, n_keys=n_keys, tq=tp, ts=ts)
        n0 = jnp.broadcast_to(n_st[:, :, None, :], (nb, M_HEADS, 8, M_DIM))
        m0 = jnp.broadcast_to(m_st[:, :, None, None], (nb, M_HEADS, 8, LANES))
        mls, c_new, n_new, m_new = _mlstm(main3, small3, gate_bias, p['mlstm_norm_g'], c_st, n0, m0,
                                          chunk=tp, valid_len=t, batches=tiles['mlstm_batches'])
        state = jnp.pad(conv_st, ((0, 0), (CONV_PAD, 0), (0, 0)))
        cnv, conv_new = _conv(main3, state, p['conv_w'], p['conv_b'], p['conv_norm_g'], p['conv_norm_b'],
                              tm=tp, valid_len=t)
        attn, mls, cnv = (a[:, :t].reshape(m, -1) for a in (attn, mls, cnv))

    x = _outproj(attn, mls, cnv, x, g1, p['w_out'], p['ln1_g'], p['ln1_b'], per_row, t, tiles['outproj'], alpha)
    x = _ffn(x, sh2, sc2, g2, p['w_ff1'], p['w_ff2'], p['ln2_g'], p['ln2_b'], per_row, t, tiles['ffn'], alpha)
    new_state = (k_new, v_new, ik_new, c_new, n_new[:, :, 0, :], m_new[:, :, 0, 0], conv_new[:, CONV_PAD:, :])
    return x, new_state


def kernel(x_prompt, x_sample, c_prompt, c_sample, cache_attn_k, cache_attn_v, cache_idx_k,
           state_mlstm_C, state_mlstm_n, state_mlstm_m, state_conv,
           w_ada, b_ada, w_in, b_igate, b_fgate, mlstm_norm_g, conv_w, conv_b,
           conv_norm_g, conv_norm_b, w_out, ln1_g, ln1_b, w_ff1, w_ff2, ln2_g, ln2_b):
    depth = w_ada.shape[0]
    alpha = (2 * depth) ** 0.25
    bp, tp, d = x_prompt.shape
    bs, tsmp, _ = x_sample.shape

    c_all = jnp.concatenate([c_prompt, c_sample], axis=0)
    rows = -(-c_all.shape[0] // 8) * 8
    ada_all = _ada(jnp.pad(c_all, ((0, rows - c_all.shape[0]), (0, 0))), w_ada, b_ada)

    xp = x_prompt.reshape(bp * tp, d)
    xs = x_sample.reshape(bs * tsmp, d)
    st_p, st_s = [], []
    for l in range(depth):
        w_main, w_small = _split_w_in(w_in[l])
        p = {'w_main': w_main, 'w_small': w_small, 'b_igate': b_igate[l], 'b_fgate': b_fgate[l],
             'mlstm_norm_g': mlstm_norm_g[l], 'conv_w': conv_w[l], 'conv_b': conv_b[l],
             'conv_norm_g': conv_norm_g[l], 'conv_norm_b': conv_norm_b[l],
             'w_out': w_out[l].astype(BF16), 'ln1_g': ln1_g[l], 'ln1_b': ln1_b[l],
             'w_ff1': w_ff1[l].astype(BF16), 'w_ff2': w_ff2[l].astype(BF16),
             'ln2_g': ln2_g[l], 'ln2_b': ln2_b[l]}
        xp, sp = _layer(xp, ada_all[l, :bp], p, None, rows_per_batch=tp, alpha=alpha)
        cache = (cache_attn_k[l], cache_attn_v[l], cache_idx_k[l], state_mlstm_C[l],
                 state_mlstm_n[l], state_mlstm_m[l], state_conv[l])
        xs, ss = _layer(xs, ada_all[l, bp:bp + bs], p, cache, rows_per_batch=tsmp, alpha=alpha)
        st_p.append(sp)
        st_s.append(ss)

    def stack(sts, i):
        return jnp.stack([s[i] for s in sts], axis=0)

    return (xp.reshape(bp, tp, d), xs.reshape(bs, tsmp, d),
            *(stack(st_p, i) for i in range(7)), *(stack(st_s, i) for i in range(7)))
```

```python
import functools

import jax
import jax.numpy as jnp
from jax import lax
from jax.experimental import pallas as pl
from jax.experimental.pallas import tpu as pltpu

F32 = jnp.float32
BF16 = jnp.bfloat16

CHUNK = 64
A_HEADS = 8
A_KV_HEADS = 2
A_GROUP = A_HEADS // A_KV_HEADS
A_HEAD_DIM = 128
A_WIDTH = A_HEADS * A_HEAD_DIM
KV_WIDTH = A_KV_HEADS * A_HEAD_DIM
IDX_HEADS = 8
IDX_DIM = 64
TOPK_MAX = 256
M_HEADS = 4
M_DIM = 128
M_WIDTH = M_HEADS * M_DIM
CONV_CH = 512
CONV_W = 31
LN_EPS = 1e-5
NEG_INF = -1e30
ATTN_LOG2E_SCALE = A_HEAD_DIM ** -0.5 * 1.4426950408889634
INT_MIN = -2 ** 31

LANES = 128
VMEM_LIMIT_BYTES = 56 * 1024 * 1024

COL_Q = 0
COL_K = COL_Q + A_WIDTH
COL_V = COL_K + KV_WIDTH
COL_IQ = COL_V + KV_WIDTH
COL_MQ = COL_IQ + IDX_HEADS * IDX_DIM
COL_MK = COL_MQ + M_WIDTH
COL_MV = COL_MK + M_WIDTH
COL_MO = COL_MV + M_WIDTH
COL_CU = COL_MO + M_WIDTH
MAIN_COLS = COL_CU + 2 * CONV_CH
SM_IK = 0
SM_IW = SM_IK + IDX_DIM
SM_MI = SM_IW + IDX_HEADS
SM_MF = SM_MI + M_HEADS
SM_USED = SM_MF + M_HEADS

SLAB = 32 * 8

CONV_HALO = 32
CONV_PAD = CONV_HALO - (CONV_W - 1)
CONV_ROWS = 32


def _cparams(sem):
    return pltpu.CompilerParams(dimension_semantics=sem, vmem_limit_bytes=VMEM_LIMIT_BYTES)


def _normalize(x):
    mu = jnp.mean(x, axis=-1, keepdims=True)
    xc = x - mu
    var = jnp.mean(xc * xc, axis=-1, keepdims=True)
    return xc * lax.rsqrt(var + LN_EPS)


def _sigmoid(x):
    return 1.0 / (1.0 + jnp.exp(-x))


def _dot(a, b):
    return jnp.dot(a, b, preferred_element_type=F32)


def _dot_nt(a, b):
    return lax.dot_general(a, b, (((1,), (1,)), ((), ())), preferred_element_type=F32)


def _ada_kernel(c_ref, w_ref, b_ref, o_ref):
    c = c_ref[...]
    s = (c * _sigmoid(c)).astype(BF16)
    o_ref[0] = _dot(s, w_ref[0].astype(BF16)) + b_ref[0]


def _ada(c_all, w_ada, b_ada):
    depth, d, n = w_ada.shape
    rows = c_all.shape[0]
    tn = 1024
    return pl.pallas_call(
        _ada_kernel,
        grid=(depth, n // tn),
        in_specs=[pl.BlockSpec((rows, d), lambda l, j: (0, 0)),
                  pl.BlockSpec((1, d, tn), lambda l, j: (l, 0, j)),
                  pl.BlockSpec((1, 1, tn), lambda l, j: (l, 0, j))],
        out_specs=pl.BlockSpec((1, rows, tn), lambda l, j: (l, 0, j)),
        out_shape=jax.ShapeDtypeStruct((depth, rows, n), F32),
        compiler_params=_cparams(("arbitrary", "arbitrary")),
        name="ada",
    )(c_all, w_ada, b_ada.reshape(depth, 1, n))


def _mod_spec(per_row, tm, rows_per_batch, d):
    if per_row:
        return pl.BlockSpec((1, tm, d), lambda i, *_: (0, i, 0))
    bpb = rows_per_batch // tm
    return pl.BlockSpec((1, 1, d), lambda i, *_: (i // bpb, 0, 0))


def _inproj_kernel(x_ref, sh_ref, sc_ref, wm_ref, ws_ref, main_ref, kv_ref, small_ref, h_scr, *, kv_block):
    j = pl.program_id(1)

    @pl.when(j == 0)
    def _():
        h = _normalize(x_ref[...]) * (1.0 + sc_ref[0]) + sh_ref[0]
        hb = h.astype(BF16)
        h_scr[...] = hb
        small_ref[...] = _dot(hb, ws_ref[...])

    acc = _dot(h_scr[...], wm_ref[...])
    main_ref[...] = acc.astype(BF16)

    @pl.when(j == kv_block)
    def _():
        kv_ref[...] = acc


def _inproj(x, sh, sc, w_main, w_small, per_row, rows_per_batch, tm):
    m, d = x.shape
    tn = 2 * KV_WIDTH
    mod = _mod_spec(per_row, tm, rows_per_batch, d)
    return pl.pallas_call(
        functools.partial(_inproj_kernel, kv_block=COL_K // tn),
        grid=(m // tm, MAIN_COLS // tn),
        in_specs=[pl.BlockSpec((tm, d), lambda i, j: (i, 0)), mod, mod,
                  pl.BlockSpec((d, tn), lambda i, j: (0, j)),
                  pl.BlockSpec((d, LANES), lambda i, j: (0, 0))],
        out_specs=[pl.BlockSpec((tm, tn), lambda i, j: (i, j)),
                   pl.BlockSpec((tm, tn), lambda i, j: (i, 0)),
                   pl.BlockSpec((tm, LANES), lambda i, j: (i, 0))],
        out_shape=[jax.ShapeDtypeStruct((m, MAIN_COLS), BF16),
                   jax.ShapeDtypeStruct((m, tn), F32),
                   jax.ShapeDtypeStruct((m, LANES), F32)],
        scratch_shapes=[pltpu.VMEM((tm, d), BF16)],
        compiler_params=_cparams(("arbitrary", "arbitrary")),
        name="inproj",
    )(x, sh, sc, w_main, w_small)


def _outproj_kernel(a_ref, m_ref, c_ref, x_ref, g_ref, w_ref, lg_ref, lb_ref, o_ref, *, alpha):
    y = _dot(a_ref[...], w_ref[0:A_WIDTH, :])
    y += _dot(m_ref[...], w_ref[A_WIDTH:A_WIDTH + M_WIDTH, :])
    y += _dot(c_ref[...], w_ref[A_WIDTH + M_WIDTH:, :])
    z = alpha * x_ref[...] + (1.0 + g_ref[0]) * y
    o_ref[...] = _normalize(z) * lg_ref[...] + lb_ref[...]


def _outproj(attn, mls, cnv, x, gate, w_out, ln_g, ln_b, per_row, rows_per_batch, tm, alpha):
    m, d = x.shape
    row = lambda w: pl.BlockSpec((tm, w), lambda i: (i, 0))
    vec = pl.BlockSpec((1, d), lambda i: (0, 0))
    return pl.pallas_call(
        functools.partial(_outproj_kernel, alpha=alpha),
        grid=(m // tm,),
        in_specs=[row(A_WIDTH), row(M_WIDTH), row(CONV_CH), row(d),
                  _mod_spec(per_row, tm, rows_per_batch, d),
                  pl.BlockSpec(w_out.shape, lambda i: (0, 0)), vec, vec],
        out_specs=row(d),
        out_shape=jax.ShapeDtypeStruct((m, d), F32),
        compiler_params=_cparams(("arbitrary",)),
        name="outproj",
    )(attn, mls, cnv, x, gate, w_out, ln_g.reshape(1, d), ln_b.reshape(1, d))


def _ffn_kernel(x_ref, sh_ref, sc_ref, g_ref, w1_ref, w2_ref, lg_ref, lb_ref, o_ref, h_scr, *, alpha):
    j = pl.program_id(1)

    @pl.when(j == 0)
    def _():
        h = _normalize(x_ref[...]) * (1.0 + sc_ref[0]) + sh_ref[0]
        h_scr[...] = h.astype(BF16)
        o_ref[...] = jnp.zeros_like(o_ref)

    u = jnp.maximum(_dot(h_scr[...], w1_ref[...]), 0.0)
    o_ref[...] += _dot((u * u).astype(BF16), w2_ref[...])

    @pl.when(j == pl.num_programs(1) - 1)
    def _():
        z = alpha * x_ref[...] + (1.0 + g_ref[0]) * o_ref[...]
        o_ref[...] = _normalize(z) * lg_ref[...] + lb_ref[...]


def _ffn(x, sh, sc, gate, w1, w2, ln_g, ln_b, per_row, rows_per_batch, tm, alpha):
    m, d = x.shape
    dff = w1.shape[1]
    tf = 512
    mod = _mod_spec(per_row, tm, rows_per_batch, d)
    vec = pl.BlockSpec((1, d), lambda i, j: (0, 0))
    return pl.pallas_call(
        functools.partial(_ffn_kernel, alpha=alpha),
        grid=(m // tm, dff // tf),
        in_specs=[pl.BlockSpec((tm, d), lambda i, j: (i, 0)), mod, mod, mod,
                  pl.BlockSpec((d, tf), lambda i, j: (0, j)),
                  pl.BlockSpec((tf, d), lambda i, j: (j, 0)), vec, vec],
        out_specs=pl.BlockSpec((tm, d), lambda i, j: (i, 0)),
        out_shape=jax.ShapeDtypeStruct((m, d), F32),
        scratch_shapes=[pltpu.VMEM((tm, d), BF16)],
        compiler_params=_cparams(("arbitrary", "arbitrary")),
        name="ffn",
    )(x, sh, sc, gate, w1, w2, ln_g.reshape(1, d), ln_b.reshape(1, d))


def _dsa_kernel(q_ref, qi_ref, wi_ref, k_ref, v_ref, ki_ref, o_ref,
                key_scr, plane_scr, alive_scr, qs_scr, qis_scr, s_scr, m_scr, l_scr, acc_scr,
                *, tq, ts, q_off, n_keys, topk, n_tiles_static):
    qb = pl.program_id(1)
    q_pos0 = q_off + qb * tq
    if n_tiles_static is None:
        vis_max = jnp.minimum(((q_pos0 + tq - 1) // CHUNK + 1) * CHUNK, n_keys)
        n_tiles = (vis_max + ts - 1) // ts
    else:
        n_tiles = n_tiles_static
    lane_tiles = [slice(c * LANES, (c + 1) * LANES) for c in range(ts // LANES)]
    n_slabs = ts // SLAB

    q_chunk = (q_pos0 + lax.broadcasted_iota(jnp.int32, (1, tq), 1)) // CHUNK
    n_vis = jnp.minimum((q_chunk + 1) * CHUNK, n_keys)
    kk = jnp.minimum(topk, n_vis).astype(F32)

    eye = (lax.broadcasted_iota(jnp.int32, (tq, tq), 0)
           == lax.broadcasted_iota(jnp.int32, (tq, tq), 1)).astype(BF16)
    for h in range(A_HEADS):
        rows = slice((h % A_GROUP) * tq, (h % A_GROUP + 1) * tq)
        qs_scr[h // A_GROUP, rows, 0:A_HEAD_DIM] = q_ref[0, :, h * A_HEAD_DIM:(h + 1) * A_HEAD_DIM]
        qs_scr[h // A_GROUP, rows, A_HEAD_DIM:A_HEAD_DIM + tq] = eye
    for h in range(IDX_HEADS):
        qis_scr[h // 2, (h % 2) * tq:(h % 2 + 1) * tq, :] = qi_ref[0, :, h * IDX_DIM:(h + 1) * IDX_DIM]
    wi = wi_ref[0] * IDX_HEADS ** -0.5 * IDX_DIM ** -0.5

    def score_tile(t):
        start = pl.multiple_of(t * ts, ts)
        kt = ki_ref[0, pl.ds(start, ts), :]
        score = None
        for pair in range(IDX_HEADS // 2):
            d = _dot_nt(kt, qis_scr[pair])
            for e in range(2):
                h = 2 * pair + e
                term = wi[h:h + 1, :] * jnp.maximum(d[:, e * tq:(e + 1) * tq], 0.0)
                score = term if score is None else score + term
        score = jnp.where(score == 0.0, 0.0, score)
        bits = lax.bitcast_convert_type(score, jnp.int32)
        key = bits ^ ((bits >> 31) & 0x7FFFFFFF)
        visible = lax.broadcasted_iota(jnp.int32, (ts, tq), 0) < n_vis - start
        key = jnp.where(visible, key, INT_MIN)
        key_scr[t] = key
        for sl in range(n_slabs):
            a = [key[sl * SLAB + 8 * j:sl * SLAB + 8 * j + 8, :] ^ INT_MIN for j in range(32)]
            dist, mask = 16, 0x0000FFFF
            while dist:
                k = 0
                while k < 32:
                    swap = (a[k] ^ lax.shift_right_logical(a[k + dist], dist)) & mask
                    a[k] = a[k] ^ swap
                    a[k + dist] = a[k + dist] ^ (swap << dist)
                    k = (k + dist + 1) & ~dist
                dist >>= 1
                mask ^= (mask << dist) & 0xFFFFFFFF
            present = a[0]
            for i in range(32):
                plane_scr[t, sl, i] = a[i]
                present = present | a[i]
            alive_scr[t, sl] = present

    def score_pair(i, carry):
        score_tile(2 * i)
        score_tile(2 * i + 1)
        return carry

    lax.fori_loop(0, n_tiles // 2, score_pair, 0)

    @pl.when(n_tiles % 2 == 1)
    def _():
        score_tile(n_tiles - 1)

    @pl.when(n_tiles % 2 == 1)
    def _():
        plane_scr[n_tiles] = jnp.zeros(plane_scr.shape[1:], jnp.int32)
        alive_scr[n_tiles] = jnp.zeros(alive_scr.shape[1:], jnp.int32)

    def sweep(i, take_prev):
        def body(pair, accs):
            accs = list(accs)
            for e in range(2):
                t = 2 * pair + e
                for sl in range(n_slabs):
                    a = alive_scr[t, sl]
                    if take_prev is not None:
                        x = a & plane_scr[t, sl, i - 1]
                        a = jnp.where(take_prev, x, a ^ x)
                        alive_scr[t, sl] = a
                    accs[e] = accs[e] + lax.population_count(a & plane_scr[t, sl, i])
            return tuple(accs)
        zero = jnp.zeros((8, tq), jnp.int32)
        acc0, acc1 = lax.fori_loop(0, (n_tiles + 1) // 2, body, (zero, zero))
        return jnp.sum((acc0 + acc1).astype(F32), axis=0, keepdims=True)

    def decide(i, cnt, want, n_alive, t_u):
        take = cnt >= want
        want = jnp.where(take, want, want - cnt)
        n_alive = jnp.where(take, cnt, n_alive - cnt)
        t_u = t_u | jnp.where(take, lax.shift_left(jnp.int32(1), 31 - i), 0)
        return take.astype(jnp.int32), want, n_alive, t_u

    def radix_step(i, carry):
        take_prev, want, n_alive, t_u = carry
        cnt = sweep(i, jnp.broadcast_to(take_prev, (8, tq)) != 0)
        return decide(i, cnt, want, n_alive, t_u)

    carry = decide(0, sweep(0, None), kk, n_vis.astype(F32), jnp.zeros((1, tq), jnp.int32))
    _, keep, n_alive, t_u = lax.fori_loop(1, 32, radix_step, carry)
    thr = jnp.broadcast_to(t_u ^ INT_MIN, (8, tq))

    @pl.when(jnp.max(n_alive - keep) > 0.0)
    def _():
        lower = (lax.broadcasted_iota(jnp.int32, (ts, ts), 0)
                 >= lax.broadcasted_iota(jnp.int32, (ts, ts), 1)).astype(BF16)

        def tie_tile(t, seen):
            blk = key_scr[t]
            eq = blk == thr[0:1, :]
            rank = seen + _dot(lower, jnp.where(eq, 1.0, 0.0).astype(BF16))
            key_scr[t] = jnp.where(eq & (rank > keep), INT_MIN, blk)
            return rank[ts - 1:ts, :]

        lax.fori_loop(0, n_tiles, tie_tile, jnp.zeros((1, tq), F32))

    m_scr[...] = jnp.full(m_scr.shape, NEG_INF, F32)
    l_scr[...] = jnp.zeros(l_scr.shape, F32)
    acc_scr[...] = jnp.zeros(acc_scr.shape, F32)

    def logits_tile(t, slot):
        start = pl.multiple_of(t * ts, ts)
        bias = jnp.where(key_scr[t] >= thr[0:1, :], 0.0, NEG_INF).astype(BF16)
        for n in range(A_KV_HEADS):
            kt = k_ref[0, pl.ds(start, ts), n * A_HEAD_DIM:(n + 1) * A_HEAD_DIM]
            s_scr[slot, n] = _dot_nt(qs_scr[n], jnp.concatenate([kt, bias], axis=1))

    def softmax_pv_tile(t, slot):
        start = pl.multiple_of(t * ts, ts)
        for n in range(A_KV_HEADS):
            vt = v_ref[0, pl.ds(start, ts), n * A_HEAD_DIM:(n + 1) * A_HEAD_DIM]
            for g in range(A_GROUP):
                h = n * A_GROUP + g
                s = [s_scr[slot, n, g * tq:(g + 1) * tq, ls] for ls in lane_tiles]
                mx = s[0]
                for sc in s[1:]:
                    mx = jnp.maximum(mx, sc)
                m_old = m_scr[h]
                m_new = jnp.maximum(m_old, jnp.max(mx, axis=1, keepdims=True))
                alpha = jnp.exp2(m_old - m_new)
                p = [jnp.exp2(sc - m_new) for sc in s]
                psum = p[0]
                for pc in p[1:]:
                    psum = psum + pc
                l_scr[h] = alpha * l_scr[h] + psum
                pv = _dot(jnp.concatenate(p, axis=1).astype(BF16), vt)
                acc_scr[h] = alpha * acc_scr[h] + pv
                m_scr[h] = m_new

    def attend_pair(i, carry):
        t = 2 * i
        logits_tile(t + 1, 1)
        softmax_pv_tile(t, 0)
        logits_tile(t + 2, 0)
        softmax_pv_tile(t + 1, 1)
        return carry

    n_pairs = (n_tiles - 1) // 2
    logits_tile(0, 0)
    lax.fori_loop(0, n_pairs, attend_pair, 0)
    t_rest = 2 * n_pairs
    softmax_pv_tile(t_rest, 0)

    @pl.when(t_rest + 1 < n_tiles)
    def _():
        logits_tile(t_rest + 1, 1)
        softmax_pv_tile(t_rest + 1, 1)

    for h in range(A_HEADS):
        l_tot = jnp.sum(l_scr[h], axis=1, keepdims=True)
        o_ref[0, :, h * A_HEAD_DIM:(h + 1) * A_HEAD_DIM] = (acc_scr[h] / l_tot).astype(BF16)


def _dsa(main3, keys_src, ki, wi, *, k_col, v_col, q_off, n_keys, tq, ts):
    assert tq == A_HEAD_DIM, "the mask rides in the unused half of a 2*A_HEAD_DIM-deep contraction"
    b, t_q, _ = main3.shape
    s_len = keys_src.shape[1]
    topk = min(TOPK_MAX, n_keys // 4)
    dynamic = q_off == 0
    kern = functools.partial(_dsa_kernel, tq=tq, ts=ts, q_off=q_off, n_keys=n_keys, topk=topk,
                             n_tiles_static=None if dynamic else s_len // ts)
    return pl.pallas_call(
        kern,
        grid=(b, t_q // tq),
        in_specs=[pl.BlockSpec((1, tq, A_WIDTH), lambda i, j: (i, j, COL_Q // A_WIDTH)),
                  pl.BlockSpec((1, tq, IDX_HEADS * IDX_DIM), lambda i, j: (i, j, COL_IQ // (IDX_HEADS * IDX_DIM))),
                  pl.BlockSpec((1, IDX_HEADS, tq), lambda i, j: (i, 0, j)),
                  pl.BlockSpec((1, s_len, KV_WIDTH), lambda i, j: (i, 0, k_col)),
                  pl.BlockSpec((1, s_len, KV_WIDTH), lambda i, j: (i, 0, v_col)),
                  pl.BlockSpec((1, s_len, IDX_DIM), lambda i, j: (i, 0, 0))],
        out_specs=pl.BlockSpec((1, tq, A_WIDTH), lambda i, j: (i, j, 0)),
        out_shape=jax.ShapeDtypeStruct((b, t_q, A_WIDTH), BF16),
        scratch_shapes=[pltpu.VMEM((s_len // ts, ts, tq), jnp.int32),
                        pltpu.VMEM((s_len // ts + 1, ts // SLAB, 32, 8, tq), jnp.int32),
                        pltpu.VMEM((s_len // ts + 1, ts // SLAB, 8, tq), jnp.int32),
                        pltpu.VMEM((A_KV_HEADS, A_GROUP * tq, A_HEAD_DIM + tq), BF16),
                        pltpu.VMEM((IDX_HEADS // 2, 2 * tq, IDX_DIM), BF16),
                        pltpu.VMEM((2, A_KV_HEADS, A_GROUP * tq, ts), F32),
                        pltpu.VMEM((A_HEADS, tq, LANES), F32),
                        pltpu.VMEM((A_HEADS, tq, LANES), F32),
                        pltpu.VMEM((A_HEADS, tq, A_HEAD_DIM), F32)],
        compiler_params=_cparams(("arbitrary", "arbitrary")),
        name="dsa",
    )(main3, main3, wi, keys_src, keys_src, ki)


def _log_sigmoid(x):
    return jnp.minimum(x, 0.0) - jnp.log1p(jnp.exp(-jnp.abs(x)))


def _mlstm_kernel(q_ref, k_ref, v_ref, o_ref, g_ref, gb_ref, ng_ref, c0_ref, n0_ref, m0_ref,
                  y_ref, c_out, n_out, m_out, c_scr, n_scr, m_scr, *, chunk, valid_len, batches):
    step = pl.program_id(1)

    @pl.when(step == 0)
    def _():
        c_scr[...] = c0_ref[...]
        n_scr[...] = n0_ref[...]
        m_scr[...] = m0_ref[...]

    for bi in range(batches):
        _mlstm_chunk(bi, q_ref, k_ref, v_ref, o_ref, g_ref, gb_ref, ng_ref, y_ref, c_scr, n_scr, m_scr,
                     chunk=chunk, valid_len=valid_len)

    c_out[...] = c_scr[...]
    n_out[...] = n_scr[...]
    m_out[...] = m_scr[...]


def _mlstm_chunk(bi, q_ref, k_ref, v_ref, o_ref, g_ref, gb_ref, ng_ref, y_ref, c_scr, n_scr, m_scr,
                 *, chunk, valid_len):
    L = chunk
    g = g_ref[bi] + gb_ref[...]
    lf = _log_sigmoid(g)
    if valid_len < L:
        live = lax.broadcasted_iota(jnp.int32, (L, 1), 0) < valid_len
        lf = jnp.where(live, lf, 0.0)
        g = jnp.where(live, g, NEG_INF)
    g_t = g.T
    lf_t = lf.T
    row = lax.broadcasted_iota(jnp.int32, (L, L), 0)
    col = lax.broadcasted_iota(jnp.int32, (L, L), 1)
    lower = row >= col
    b_c = jnp.dot(jnp.where(lower, 1.0, 0.0), lf, precision=lax.Precision.HIGHEST,
                  preferred_element_type=F32)
    b_r = jnp.dot(lf_t, jnp.where(lower, 0.0, 1.0) + jnp.where(row == col, 1.0, 0.0),
                  precision=lax.Precision.HIGHEST, preferred_element_type=F32)

    for h in range(M_HEADS):
        sl = slice(h * M_DIM, (h + 1) * M_DIM)
        b_col = b_c[:, SM_MF + h:SM_MF + h + 1]
        ig_col = g[:, SM_MI + h:SM_MI + h + 1]
        a_row = g_t[SM_MI + h:SM_MI + h + 1, :] - b_r[SM_MF + h:SM_MF + h + 1, :]
        m_prev = m_scr[bi, h, 0:1, 0:1]
        d_mat = jnp.where(lower, b_col + a_row, NEG_INF)
        inter = b_col + m_prev
        m_t = jnp.maximum(inter, jnp.max(d_mat, axis=1, keepdims=True))
        w_intra = jnp.exp(d_mat - m_t)
        w_inter = jnp.exp(inter - m_t)
        qh = q_ref[bi, :, sl]
        kf = k_ref[bi, :, sl].astype(F32) * M_DIM ** -0.5
        vh = v_ref[bi, :, sl]
        s = _dot_nt(qh, kf.astype(BF16)) * w_intra
        c_old = c_scr[bi, h]
        n_old = n_scr[bi, h, 0:1, :]
        num = w_inter * _dot(qh, c_old.astype(BF16)) + _dot(s.astype(BF16), vh)
        den = (w_inter * jnp.sum(qh.astype(F32) * n_old, axis=1, keepdims=True)
               + jnp.sum(s, axis=1, keepdims=True))
        hid = num / jnp.maximum(jnp.abs(den), jnp.exp(-m_t))
        m_end = m_t[L - 1:L, :]
        b_last = b_col[L - 1:L, :]
        w_end = jnp.exp(b_last - b_col + ig_col - m_end)
        decay = jnp.exp(b_last + m_prev - m_end)
        kw = w_end * kf
        c_scr[bi, h] = decay * c_old + _dot(kw.T.astype(BF16), vh)
        n_scr[bi, h] = jnp.broadcast_to(decay * n_old + jnp.sum(kw, axis=0, keepdims=True), (8, M_DIM))
        m_scr[bi, h] = jnp.broadcast_to(m_end, (8, LANES))
        gated = _sigmoid(o_ref[bi, :, sl].astype(F32)) * hid
        y_ref[bi, :, sl] = (_normalize(gated) * ng_ref[:, sl]).astype(BF16)


def _mlstm(main3, small3, gate_bias, norm_g, c0, n0, m0, *, chunk, valid_len, batches):
    b, t, _ = main3.shape
    bb = batches
    blk = lambda col: pl.BlockSpec((bb, chunk, M_WIDTH), lambda i, j: (i, j, col // M_WIDTH))
    st = lambda shape: pl.BlockSpec((bb,) + shape, lambda i, j: (i,) + (0,) * len(shape))
    c_shape, n_shape, m_shape = (M_HEADS, M_DIM, M_DIM), (M_HEADS, 8, M_DIM), (M_HEADS, 8, LANES)
    return pl.pallas_call(
        functools.partial(_mlstm_kernel, chunk=chunk, valid_len=valid_len, batches=bb),
        grid=(b // bb, t // chunk),
        in_specs=[blk(COL_MQ), blk(COL_MK), blk(COL_MV), blk(COL_MO),
                  pl.BlockSpec((bb, chunk, LANES), lambda i, j: (i, j, 0)),
                  pl.BlockSpec((1, LANES), lambda i, j: (0, 0)),
                  pl.BlockSpec((1, M_WIDTH), lambda i, j: (0, 0)),
                  st(c_shape), st(n_shape), st(m_shape)],
        out_specs=[pl.BlockSpec((bb, chunk, M_WIDTH), lambda i, j: (i, j, 0)),
                   st(c_shape), st(n_shape), st(m_shape)],
        out_shape=[jax.ShapeDtypeStruct((b, t, M_WIDTH), BF16),
                   jax.ShapeDtypeStruct((b,) + c_shape, F32),
                   jax.ShapeDtypeStruct((b,) + n_shape, F32),
                   jax.ShapeDtypeStruct((b,) + m_shape, F32)],
        scratch_shapes=[pltpu.VMEM((bb,) + c_shape, F32), pltpu.VMEM((bb,) + n_shape, F32),
                        pltpu.VMEM((bb,) + m_shape, F32)],
        compiler_params=_cparams(("arbitrary", "arbitrary")),
        name="mlstm",
    )(main3, main3, main3, main3, small3, gate_bias, norm_g.reshape(1, M_WIDTH), c0, n0, m0)


def _conv_kernel(cu_ref, halo_ref, st_ref, w_ref, cb_ref, ng_ref, nb_ref, y_ref, ns_ref, full_scr, sh_scr,
                 *, tm, valid_len):
    i = pl.program_id(1)

    def glu(u):
        return u[:, :CONV_CH].astype(F32) * _sigmoid(u[:, CONV_CH:].astype(F32))

    @pl.when(i == 0)
    def _():
        full_scr[0:CONV_HALO, :] = st_ref[0]

    @pl.when(i > 0)
    def _():
        full_scr[0:CONV_HALO, :] = glu(halo_ref[0])

    full_scr[CONV_HALO:CONV_HALO + tm, :] = glu(cu_ref[0])
    span = tm + CONV_HALO - 8
    for b in range(1, 8):
        sh_scr[b - 1] = full_scr[pl.ds(b, span), :]
    for c0 in range(0, tm, CONV_ROWS):
        acc = None
        for j in range(CONV_W):
            a8, b = (CONV_PAD + j) // 8 * 8, (CONV_PAD + j) % 8
            if b == 0:
                src = full_scr[pl.ds(c0 + a8, CONV_ROWS), :]
            else:
                src = sh_scr[b - 1, pl.ds(c0 + a8, CONV_ROWS), :]
            term = w_ref[j:j + 1, :] * src
            acc = term if acc is None else acc + term
        y = _normalize(acc + cb_ref[...]) * ng_ref[...] + nb_ref[...]
        y_ref[0, c0:c0 + CONV_ROWS, :] = (y * _sigmoid(y)).astype(BF16)
    ns_ref[0] = full_scr[pl.ds(valid_len, CONV_HALO), :]


def _conv(main3, state, w, bias, ng, nb, *, tm, valid_len):
    b, t, _ = main3.shape
    cu_blk = COL_CU // (2 * CONV_CH)
    hpb = tm // CONV_HALO
    vec = pl.BlockSpec((1, CONV_CH), lambda i, j: (0, 0))
    return pl.pallas_call(
        functools.partial(_conv_kernel, tm=tm, valid_len=valid_len),
        grid=(b, t // tm),
        in_specs=[pl.BlockSpec((1, tm, 2 * CONV_CH), lambda i, j: (i, j, cu_blk)),
                  pl.BlockSpec((1, CONV_HALO, 2 * CONV_CH), lambda i, j: (i, jnp.maximum(j * hpb - 1, 0), cu_blk)),
                  pl.BlockSpec((1, CONV_HALO, CONV_CH), lambda i, j: (i, 0, 0)),
                  pl.BlockSpec((CONV_W, CONV_CH), lambda i, j: (0, 0)), vec, vec, vec],
        out_specs=[pl.BlockSpec((1, tm, CONV_CH), lambda i, j: (i, j, 0)),
                   pl.BlockSpec((1, CONV_HALO, CONV_CH), lambda i, j: (i, 0, 0))],
        out_shape=[jax.ShapeDtypeStruct((b, t, CONV_CH), BF16),
                   jax.ShapeDtypeStruct((b, CONV_HALO, CONV_CH), F32)],
        scratch_shapes=[pltpu.VMEM((CONV_HALO + tm, CONV_CH), F32),
                        pltpu.VMEM((7, CONV_HALO + tm - 8, CONV_CH), F32)],
        compiler_params=_cparams(("arbitrary", "arbitrary")),
        name="conv",
    )(main3, main3, state, w, bias.reshape(1, CONV_CH), ng.reshape(1, CONV_CH), nb.reshape(1, CONV_CH))


def _split_w_in(w_in):
    sizes = (A_WIDTH, KV_WIDTH, KV_WIDTH, IDX_HEADS * IDX_DIM, IDX_DIM, IDX_HEADS,
             M_WIDTH, M_WIDTH, M_WIDTH, M_HEADS, M_HEADS, M_WIDTH, 2 * CONV_CH)
    parts, start = [], 0
    for size in sizes:
        parts.append(w_in[:, start:start + size])
        start += size
    aq, ak, av, iq, ik, iw, mq, mk, mv, mi, mf, mo, cu = parts
    aq = aq * ATTN_LOG2E_SCALE
    w_main = jnp.concatenate([aq, ak, av, iq, mq, mk, mv, mo, cu], axis=1).astype(BF16)
    pad = jnp.zeros((w_in.shape[0], LANES - SM_USED), w_in.dtype)
    w_small = jnp.concatenate([ik, iw, mi, mf, pad], axis=1).astype(BF16)
    return w_main, w_small


def _tiles(m, rows_per_batch, per_row):
    if per_row:
        return dict(inproj=m, outproj=m, ffn=m, dsa_q=128, dsa_k=512, mlstm=128, mlstm_batches=1, conv=128)
    t = rows_per_batch
    return dict(inproj=min(1024, t), outproj=min(512, t), ffn=min(1024, t),
                dsa_q=min(128, t), dsa_k=min(512, t), mlstm=min(256, t), mlstm_batches=1, conv=min(512, t))


def _layer(x, ada, p, cache, *, rows_per_batch, alpha):
    m, d = x.shape
    nb = m // rows_per_batch
    t = rows_per_batch
    per_row = cache is not None
    tiles = _tiles(m, t, per_row)
    if per_row:
        mods = [jnp.repeat(v, t, axis=0)[None] for v in jnp.split(ada, 6, axis=-1)]
    else:
        mods = [v[:, None, :] for v in jnp.split(ada, 6, axis=-1)]
    sh1, sc1, g1, sh2, sc2, g2 = mods

    main, kv, small = _inproj(x, sh1, sc1, p['w_main'], p['w_small'], per_row, t, tiles['inproj'])
    k_new = kv[:, :KV_WIDTH].reshape(nb, t, A_KV_HEADS, A_HEAD_DIM)
    v_new = kv[:, KV_WIDTH:].reshape(nb, t, A_KV_HEADS, A_HEAD_DIM)
    ik_new = small[:, SM_IK:SM_IK + IDX_DIM].reshape(nb, t, IDX_DIM)

    gate_bias = jnp.zeros((1, LANES), F32)
    gate_bias = gate_bias.at[0, SM_MI:SM_MI + M_HEADS].set(p['b_igate'])
    gate_bias = gate_bias.at[0, SM_MF:SM_MF + M_HEADS].set(p['b_fgate'])

    if cache is None:
        main3 = main.reshape(nb, t, MAIN_COLS)
        small3 = small.reshape(nb, t, LANES)
        ki = ik_new.astype(BF16)
        wi = jnp.swapaxes(small3[:, :, SM_IW:SM_IW + IDX_HEADS], 1, 2)
        attn = _dsa(main3, main3, ki, wi, k_col=COL_K // KV_WIDTH, v_col=COL_V // KV_WIDTH,
                    q_off=0, n_keys=t, tq=tiles['dsa_q'], ts=tiles['dsa_k'])
        c0 = jnp.zeros((nb, M_HEADS, M_DIM, M_DIM), F32)
        n0 = jnp.zeros((nb, M_HEADS, 8, M_DIM), F32)
        m0 = jnp.zeros((nb, M_HEADS, 8, LANES), F32)
        mls, c_new, n_new, m_new = _mlstm(main3, small3, gate_bias, p['mlstm_norm_g'], c0, n0, m0,
                                          chunk=tiles['mlstm'], valid_len=tiles['mlstm'],
                                          batches=tiles['mlstm_batches'])
        state = jnp.zeros((nb, CONV_HALO, CONV_CH), F32)
        cnv, conv_new = _conv(main3, state, p['conv_w'], p['conv_b'], p['conv_norm_g'], p['conv_norm_b'],
                              tm=tiles['conv'], valid_len=tiles['conv'])
        attn, mls, cnv = (a.reshape(m, -1) for a in (attn, mls, cnv))
    else:
        ck, cv, cik, c_st, n_st, m_st, conv_st = cache
        past = ck.shape[1]
        n_keys = past + t
        tp = tiles['dsa_q']
        ts = tiles['dsa_k']
        s_pad = -(-n_keys // ts) * ts
        main3 = jnp.pad(main.reshape(nb, t, MAIN_COLS), ((0, 0), (0, tp - t), (0, 0)))
        small3 = jnp.pad(small.reshape(nb, t, LANES), ((0, 0), (0, tp - t), (0, 0)))
        pad_keys = lambda a: jnp.pad(a, ((0, 0), (0, s_pad - n_keys), (0, 0)))
        k_all = pad_keys(jnp.concatenate([ck.reshape(nb, past, KV_WIDTH), kv[:, :KV_WIDTH].reshape(nb, t, KV_WIDTH)], axis=1))
        v_all = pad_keys(jnp.concatenate([cv.reshape(nb, past, KV_WIDTH), kv[:, KV_WIDTH:].reshape(nb, t, KV_WIDTH)], axis=1))
        kv_all = jnp.concatenate([k_all, v_all], axis=-1).astype(BF16)
        ki = pad_keys(jnp.concatenate([cik, ik_new], axis=1)).astype(BF16)
        wi = jnp.swapaxes(small3[:, :, SM_IW:SM_IW + IDX_HEADS], 1, 2)
        attn = _dsa(main3, kv_all, ki, wi, k_col=0, v_col=1, q_off=past, n_keys=n_keys, tq=tp, ts=ts)
        n0 = jnp.broadcast_to(n_st[:, :, None, :], (nb, M_HEADS, 8, M_DIM))
        m0 = jnp.broadcast_to(m_st[:, :, None, None], (nb, M_HEADS, 8, LANES))
        mls, c_new, n_new, m_new = _mlstm(main3, small3, gate_bias, p['mlstm_norm_g'], c_st, n0, m0,
                                          chunk=tp, valid_len=t, batches=tiles['mlstm_batches'])
        state = jnp.pad(conv_st, ((0, 0), (CONV_PAD, 0), (0, 0)))
        cnv, conv_new = _conv(main3, state, p['conv_w'], p['conv_b'], p['conv_norm_g'], p['conv_norm_b'],
                              tm=tp, valid_len=t)
        attn, mls, cnv = (a[:, :t].reshape(m, -1) for a in (attn, mls, cnv))

    x = _outproj(attn, mls, cnv, x, g1, p['w_out'], p['ln1_g'], p['ln1_b'], per_row, t, tiles['outproj'], alpha)
    x = _ffn(x, sh2, sc2, g2, p['w_ff1'], p['w_ff2'], p['ln2_g'], p['ln2_b'], per_row, t, tiles['ffn'], alpha)
    new_state = (k_new, v_new, ik_new, c_new, n_new[:, :, 0, :], m_new[:, :, 0, 0], conv_new[:, CONV_PAD:, :])
    return x, new_state


def kernel(x_prompt, x_sample, c_prompt, c_sample, cache_attn_k, cache_attn_v, cache_idx_k,
           state_mlstm_C, state_mlstm_n, state_mlstm_m, state_conv,
           w_ada, b_ada, w_in, b_igate, b_fgate, mlstm_norm_g, conv_w, conv_b,
           conv_norm_g, conv_norm_b, w_out, ln1_g, ln1_b, w_ff1, w_ff2, ln2_g, ln2_b):
    depth = w_ada.shape[0]
    alpha = (2 * depth) ** 0.25
    bp, tp, d = x_prompt.shape
    bs, tsmp, _ = x_sample.shape

    c_all = jnp.concatenate([c_prompt, c_sample], axis=0)
    rows = -(-c_all.shape[0] // 8) * 8
    ada_all = _ada(jnp.pad(c_all, ((0, rows - c_all.shape[0]), (0, 0))), w_ada, b_ada)

    xp = x_prompt.reshape(bp * tp, d)
    xs = x_sample.reshape(bs * tsmp, d)
    st_p, st_s = [], []
    for l in range(depth):
        w_main, w_small = _split_w_in(w_in[l])
        p = {'w_main': w_main, 'w_small': w_small, 'b_igate': b_igate[l], 'b_fgate': b_fgate[l],
             'mlstm_norm_g': mlstm_norm_g[l], 'conv_w': conv_w[l], 'conv_b': conv_b[l],
             'conv_norm_g': conv_norm_g[l], 'conv_norm_b': conv_norm_b[l],
             'w_out': w_out[l].astype(BF16), 'ln1_g': ln1_g[l], 'ln1_b': ln1_b[l],
             'w_ff1': w_ff1[l].astype(BF16), 'w_ff2': w_ff2[l].astype(BF16),
             'ln2_g': ln2_g[l], 'ln2_b': ln2_b[l]}
        xp, sp = _layer(xp, ada_all[l, :bp], p, None, rows_per_batch=tp, alpha=alpha)
        cache = (cache_attn_k[l], cache_attn_v[l], cache_idx_k[l], state_mlstm_C[l],
                 state_mlstm_n[l], state_mlstm_m[l], state_conv[l])
        xs, ss = _layer(xs, ada_all[l, bp:bp + bs], p, cache, rows_per_batch=tsmp, alpha=alpha)
        st_p.append(sp)
        st_s.append(ss)

    def stack(sts, i):
        return jnp.stack([s[i] for s in sts], axis=0)

    return (xp.reshape(bp, tp, d), xs.reshape(bs, tsmp, d),
            *(stack(st_p, i) for i in range(7)), *(stack(st_s, i) for i in range(7)))
```

```python
import functools

import jax
import jax.numpy as jnp
from jax import lax
from jax.experimental import pallas as pl
from jax.experimental.pallas import tpu as pltpu

F32 = jnp.float32
BF16 = jnp.bfloat16

CHUNK = 64
A_HEADS = 8
A_KV_HEADS = 2
A_GROUP = A_HEADS // A_KV_HEADS
A_HEAD_DIM = 128
A_WIDTH = A_HEADS * A_HEAD_DIM
KV_WIDTH = A_KV_HEADS * A_HEAD_DIM
IDX_HEADS = 8
IDX_DIM = 64
TOPK_MAX = 256
M_HEADS = 4
M_DIM = 128
M_WIDTH = M_HEADS * M_DIM
CONV_CH = 512
CONV_W = 31
LN_EPS = 1e-5
NEG_INF = -1e30
ATTN_LOG2E_SCALE = A_HEAD_DIM ** -0.5 * 1.4426950408889634
INT_MIN = -2 ** 31

LANES = 128
VMEM_LIMIT_BYTES = 56 * 1024 * 1024

COL_Q = 0
COL_K = COL_Q + A_WIDTH
COL_V = COL_K + KV_WIDTH
COL_IQ = COL_V + KV_WIDTH
COL_MQ = COL_IQ + IDX_HEADS * IDX_DIM
COL_MK = COL_MQ + M_WIDTH
COL_MV = COL_MK + M_WIDTH
COL_MO = COL_MV + M_WIDTH
COL_CU = COL_MO + M_WIDTH
MAIN_COLS = COL_CU + 2 * CONV_CH
SM_IK = 0
SM_IW = SM_IK + IDX_DIM
SM_MI = SM_IW + IDX_HEADS
SM_MF = SM_MI + M_HEADS
SM_USED = SM_MF + M_HEADS

SLAB = 32 * 8

CONV_HALO = 32
CONV_PAD = CONV_HALO - (CONV_W - 1)
CONV_ROWS = 32


def _cparams(sem):
    return pltpu.CompilerParams(dimension_semantics=sem, vmem_limit_bytes=VMEM_LIMIT_BYTES)


def _normalize(x):
    mu = jnp.mean(x, axis=-1, keepdims=True)
    xc = x - mu
    var = jnp.mean(xc * xc, axis=-1, keepdims=True)
    return xc * lax.rsqrt(var + LN_EPS)


def _sigmoid(x):
    return 1.0 / (1.0 + jnp.exp(-x))


def _dot(a, b):
    return jnp.dot(a, b, preferred_element_type=F32)


def _dot_nt(a, b):
    return lax.dot_general(a, b, (((1,), (1,)), ((), ())), preferred_element_type=F32)


def _ada_kernel(c_ref, w_ref, b_ref, o_ref):
    c = c_ref[...]
    s = (c * _sigmoid(c)).astype(BF16)
    o_ref[0] = _dot(s, w_ref[0].astype(BF16)) + b_ref[0]


def _ada(c_all, w_ada, b_ada):
    depth, d, n = w_ada.shape
    rows = c_all.shape[0]
    tn = 1024
    return pl.pallas_call(
        _ada_kernel,
        grid=(depth, n // tn),
        in_specs=[pl.BlockSpec((rows, d), lambda l, j: (0, 0)),
                  pl.BlockSpec((1, d, tn), lambda l, j: (l, 0, j)),
                  pl.BlockSpec((1, 1, tn), lambda l, j: (l, 0, j))],
        out_specs=pl.BlockSpec((1, rows, tn), lambda l, j: (l, 0, j)),
        out_shape=jax.ShapeDtypeStruct((depth, rows, n), F32),
        compiler_params=_cparams(("arbitrary", "arbitrary")),
        name="ada",
    )(c_all, w_ada, b_ada.reshape(depth, 1, n))


def _mod_spec(per_row, tm, rows_per_batch, d):
    if per_row:
        return pl.BlockSpec((1, tm, d), lambda i, *_: (0, i, 0))
    bpb = rows_per_batch // tm
    return pl.BlockSpec((1, 1, d), lambda i, *_: (i // bpb, 0, 0))


def _inproj_kernel(x_ref, sh_ref, sc_ref, wm_ref, ws_ref, main_ref, k_ref, v_ref, small_ref, h_scr, *, kv_block):
    j = pl.program_id(1)

    @pl.when(j == 0)
    def _():
        h = _normalize(x_ref[...]) * (1.0 + sc_ref[0]) + sh_ref[0]
        hb = h.astype(BF16)
        h_scr[...] = hb
        small_ref[...] = _dot(hb, ws_ref[...])

    acc = _dot(h_scr[...], wm_ref[...])
    main_ref[...] = acc.astype(BF16)

    @pl.when(j == kv_block)
    def _():
        k_ref[...] = acc[:, :KV_WIDTH]
        v_ref[...] = acc[:, KV_WIDTH:]


def _inproj(x, sh, sc, w_main, w_small, per_row, rows_per_batch, tm):
    m, d = x.shape
    tn = 2 * KV_WIDTH
    mod = _mod_spec(per_row, tm, rows_per_batch, d)
    return pl.pallas_call(
        functools.partial(_inproj_kernel, kv_block=COL_K // tn),
        grid=(m // tm, MAIN_COLS // tn),
        in_specs=[pl.BlockSpec((tm, d), lambda i, j: (i, 0)), mod, mod,
                  pl.BlockSpec((d, tn), lambda i, j: (0, j)),
                  pl.BlockSpec((d, LANES), lambda i, j: (0, 0))],
        out_specs=[pl.BlockSpec((tm, tn), lambda i, j: (i, j)),
                   pl.BlockSpec((tm, KV_WIDTH), lambda i, j: (i, 0)),
                   pl.BlockSpec((tm, KV_WIDTH), lambda i, j: (i, 0)),
                   pl.BlockSpec((tm, LANES), lambda i, j: (i, 0))],
        out_shape=[jax.ShapeDtypeStruct((m, MAIN_COLS), BF16),
                   jax.ShapeDtypeStruct((m, KV_WIDTH), F32),
                   jax.ShapeDtypeStruct((m, KV_WIDTH), F32),
                   jax.ShapeDtypeStruct((m, LANES), F32)],
        scratch_shapes=[pltpu.VMEM((tm, d), BF16)],
        compiler_params=_cparams(("arbitrary", "arbitrary")),
        name="inproj",
    )(x, sh, sc, w_main, w_small)


def _outproj_kernel(a_ref, m_ref, c_ref, x_ref, g_ref, w_ref, lg_ref, lb_ref, o_ref, *, alpha):
    y = _dot(a_ref[...], w_ref[0:A_WIDTH, :])
    y += _dot(m_ref[...], w_ref[A_WIDTH:A_WIDTH + M_WIDTH, :])
    y += _dot(c_ref[...], w_ref[A_WIDTH + M_WIDTH:, :])
    z = alpha * x_ref[...] + (1.0 + g_ref[0]) * y
    o_ref[...] = _normalize(z) * lg_ref[...] + lb_ref[...]


def _outproj(attn, mls, cnv, x, gate, w_out, ln_g, ln_b, per_row, rows_per_batch, tm, alpha):
    m, d = x.shape
    row = lambda w: pl.BlockSpec((tm, w), lambda i: (i, 0))
    vec = pl.BlockSpec((1, d), lambda i: (0, 0))
    return pl.pallas_call(
        functools.partial(_outproj_kernel, alpha=alpha),
        grid=(m // tm,),
        in_specs=[row(A_WIDTH), row(M_WIDTH), row(CONV_CH), row(d),
                  _mod_spec(per_row, tm, rows_per_batch, d),
                  pl.BlockSpec(w_out.shape, lambda i: (0, 0)), vec, vec],
        out_specs=row(d),
        out_shape=jax.ShapeDtypeStruct((m, d), F32),
        compiler_params=_cparams(("arbitrary",)),
        name="outproj",
    )(attn, mls, cnv, x, gate, w_out, ln_g.reshape(1, d), ln_b.reshape(1, d))


def _ffn_kernel(x_ref, sh_ref, sc_ref, g_ref, w1_ref, w2_ref, lg_ref, lb_ref, o_ref, h_scr, *, alpha):
    j = pl.program_id(1)

    @pl.when(j == 0)
    def _():
        h = _normalize(x_ref[...]) * (1.0 + sc_ref[0]) + sh_ref[0]
        h_scr[...] = h.astype(BF16)
        o_ref[...] = jnp.zeros_like(o_ref)

    u = jnp.maximum(_dot(h_scr[...], w1_ref[...]), 0.0)
    o_ref[...] += _dot((u * u).astype(BF16), w2_ref[...])

    @pl.when(j == pl.num_programs(1) - 1)
    def _():
        z = alpha * x_ref[...] + (1.0 + g_ref[0]) * o_ref[...]
        o_ref[...] = _normalize(z) * lg_ref[...] + lb_ref[...]


def _ffn(x, sh, sc, gate, w1, w2, ln_g, ln_b, per_row, rows_per_batch, tm, alpha):
    m, d = x.shape
    dff = w1.shape[1]
    tf = 512
    mod = _mod_spec(per_row, tm, rows_per_batch, d)
    vec = pl.BlockSpec((1, d), lambda i, j: (0, 0))
    return pl.pallas_call(
        functools.partial(_ffn_kernel, alpha=alpha),
        grid=(m // tm, dff // tf),
        in_specs=[pl.BlockSpec((tm, d), lambda i, j: (i, 0)), mod, mod, mod,
                  pl.BlockSpec((d, tf), lambda i, j: (0, j)),
                  pl.BlockSpec((tf, d), lambda i, j: (j, 0)), vec, vec],
        out_specs=pl.BlockSpec((tm, d), lambda i, j: (i, 0)),
        out_shape=jax.ShapeDtypeStruct((m, d), F32),
        scratch_shapes=[pltpu.VMEM((tm, d), BF16)],
        compiler_params=_cparams(("arbitrary", "arbitrary")),
        name="ffn",
    )(x, sh, sc, gate, w1, w2, ln_g.reshape(1, d), ln_b.reshape(1, d))


def _dsa_kernel(q_ref, qi_ref, wi_ref, k_ref, v_ref, ki_ref, o_ref,
                key_scr, plane_scr, alive_scr, qs_scr, qis_scr, s_scr, m_scr, l_scr, acc_scr,
                *, tq, ts, q_off, n_keys, topk, n_tiles_static):
    qb = pl.program_id(1)
    q_pos0 = q_off + qb * tq
    if n_tiles_static is None:
        vis_max = jnp.minimum(((q_pos0 + tq - 1) // CHUNK + 1) * CHUNK, n_keys)
        n_tiles = (vis_max + ts - 1) // ts
    else:
        n_tiles = n_tiles_static
    lane_tiles = [slice(c * LANES, (c + 1) * LANES) for c in range(ts // LANES)]
    n_slabs = ts // SLAB

    q_chunk = (q_pos0 + lax.broadcasted_iota(jnp.int32, (1, tq), 1)) // CHUNK
    n_vis = jnp.minimum((q_chunk + 1) * CHUNK, n_keys)
    kk = jnp.minimum(topk, n_vis).astype(F32)

    eye = (lax.broadcasted_iota(jnp.int32, (tq, tq), 0)
           == lax.broadcasted_iota(jnp.int32, (tq, tq), 1)).astype(BF16)
    for h in range(A_HEADS):
        rows = slice((h % A_GROUP) * tq, (h % A_GROUP + 1) * tq)
        qs_scr[h // A_GROUP, rows, 0:A_HEAD_DIM] = q_ref[0, :, h * A_HEAD_DIM:(h + 1) * A_HEAD_DIM]
        qs_scr[h // A_GROUP, rows, A_HEAD_DIM:A_HEAD_DIM + tq] = eye
    for h in range(IDX_HEADS):
        qis_scr[h // 2, (h % 2) * tq:(h % 2 + 1) * tq, :] = qi_ref[0, :, h * IDX_DIM:(h + 1) * IDX_DIM]
    wi = wi_ref[0] * IDX_HEADS ** -0.5 * IDX_DIM ** -0.5

    def score_tile(t):
        start = pl.multiple_of(t * ts, ts)
        kt = ki_ref[0, pl.ds(start, ts), :]
        score = None
        for pair in range(IDX_HEADS // 2):
            d = _dot_nt(kt, qis_scr[pair])
            for e in range(2):
                h = 2 * pair + e
                term = wi[h:h + 1, :] * jnp.maximum(d[:, e * tq:(e + 1) * tq], 0.0)
                score = term if score is None else score + term
        score = jnp.where(score == 0.0, 0.0, score)
        bits = lax.bitcast_convert_type(score, jnp.int32)
        key = bits ^ ((bits >> 31) & 0x7FFFFFFF)
        visible = lax.broadcasted_iota(jnp.int32, (ts, tq), 0) < n_vis - start
        key = jnp.where(visible, key, INT_MIN)
        key_scr[t] = key
        for sl in range(n_slabs):
            a = [key[sl * SLAB + 8 * j:sl * SLAB + 8 * j + 8, :] ^ INT_MIN for j in range(32)]
            dist, mask = 16, 0x0000FFFF
            while dist:
                k = 0
                while k < 32:
                    swap = (a[k] ^ lax.shift_right_logical(a[k + dist], dist)) & mask
                    a[k] = a[k] ^ swap
                    a[k + dist] = a[k + dist] ^ (swap << dist)
                    k = (k + dist + 1) & ~dist
                dist >>= 1
                mask ^= (mask << dist) & 0xFFFFFFFF
            present = a[0]
            for i in range(32):
                plane_scr[t, sl, i] = a[i]
                present = present | a[i]
            alive_scr[t, sl] = present

    def score_pair(i, carry):
        score_tile(2 * i)
        score_tile(2 * i + 1)
        return carry

    lax.fori_loop(0, n_tiles // 2, score_pair, 0)

    @pl.when(n_tiles % 2 == 1)
    def _():
        score_tile(n_tiles - 1)

    @pl.when(n_tiles % 2 == 1)
    def _():
        plane_scr[n_tiles] = jnp.zeros(plane_scr.shape[1:], jnp.int32)
        alive_scr[n_tiles] = jnp.zeros(alive_scr.shape[1:], jnp.int32)

    def sweep(i, take_prev):
        def body(pair, accs):
            accs = list(accs)
            for e in range(2):
                t = 2 * pair + e
                for sl in range(n_slabs):
                    a = alive_scr[t, sl]
                    if take_prev is not None:
                        x = a & plane_scr[t, sl, i - 1]
                        a = jnp.where(take_prev, x, a ^ x)
                        alive_scr[t, sl] = a
                    accs[e] = accs[e] + lax.population_count(a & plane_scr[t, sl, i])
            return tuple(accs)
        zero = jnp.zeros((8, tq), jnp.int32)
        acc0, acc1 = lax.fori_loop(0, (n_tiles + 1) // 2, body, (zero, zero))
        return jnp.sum((acc0 + acc1).astype(F32), axis=0, keepdims=True)

    def decide(i, cnt, want, n_alive, t_u):
        take = cnt >= want
        want = jnp.where(take, want, want - cnt)
        n_alive = jnp.where(take, cnt, n_alive - cnt)
        t_u = t_u | jnp.where(take, lax.shift_left(jnp.int32(1), 31 - i), 0)
        return take.astype(jnp.int32), want, n_alive, t_u

    def radix_step(i, carry):
        take_prev, want, n_alive, t_u = carry
        cnt = sweep(i, jnp.broadcast_to(take_prev, (8, tq)) != 0)
        return decide(i, cnt, want, n_alive, t_u)

    carry = decide(0, sweep(0, None), kk, n_vis.astype(F32), jnp.zeros((1, tq), jnp.int32))
    _, keep, n_alive, t_u = lax.fori_loop(1, 32, radix_step, carry)
    thr = jnp.broadcast_to(t_u ^ INT_MIN, (8, tq))

    @pl.when(jnp.max(n_alive - keep) > 0.0)
    def _():
        lower = (lax.broadcasted_iota(jnp.int32, (ts, ts), 0)
                 >= lax.broadcasted_iota(jnp.int32, (ts, ts), 1)).astype(BF16)

        def tie_tile(t, seen):
            blk = key_scr[t]
            eq = blk == thr[0:1, :]
            rank = seen + _dot(lower, jnp.where(eq, 1.0, 0.0).astype(BF16))
            key_scr[t] = jnp.where(eq & (rank > keep), INT_MIN, blk)
            return rank[ts - 1:ts, :]

        lax.fori_loop(0, n_tiles, tie_tile, jnp.zeros((1, tq), F32))

    m_scr[...] = jnp.full(m_scr.shape, NEG_INF, F32)
    l_scr[...] = jnp.zeros(l_scr.shape, F32)
    acc_scr[...] = jnp.zeros(acc_scr.shape, F32)

    def logits_tile(t, slot):
        start = pl.multiple_of(t * ts, ts)
        bias = jnp.where(key_scr[t] >= thr[0:1, :], 0.0, NEG_INF).astype(BF16)
        for n in range(A_KV_HEADS):
            kt = k_ref[0, pl.ds(start, ts), n * A_HEAD_DIM:(n + 1) * A_HEAD_DIM]
            s_scr[slot, n] = _dot_nt(qs_scr[n], jnp.concatenate([kt, bias], axis=1))

    def softmax_pv_tile(t, slot):
        start = pl.multiple_of(t * ts, ts)
        for n in range(A_KV_HEADS):
            vt = v_ref[0, pl.ds(start, ts), n * A_HEAD_DIM:(n + 1) * A_HEAD_DIM]
            for g in range(A_GROUP):
                h = n * A_GROUP + g
                s = [s_scr[slot, n, g * tq:(g + 1) * tq, ls] for ls in lane_tiles]
                mx = s[0]
                for sc in s[1:]:
                    mx = jnp.maximum(mx, sc)
                m_old = m_scr[h]
                m_new = jnp.maximum(m_old, jnp.max(mx, axis=1, keepdims=True))
                alpha = jnp.exp2(m_old - m_new)
                p = [jnp.exp2(sc - m_new) for sc in s]
                psum = p[0]
                for pc in p[1:]:
                    psum = psum + pc
                l_scr[h] = alpha * l_scr[h] + psum
                pv = _dot(jnp.concatenate(p, axis=1).astype(BF16), vt)
                acc_scr[h] = alpha * acc_scr[h] + pv
                m_scr[h] = m_new

    def attend_pair(i, carry):
        t = 2 * i
        logits_tile(t + 1, 1)
        softmax_pv_tile(t, 0)
        logits_tile(t + 2, 0)
        softmax_pv_tile(t + 1, 1)
        return carry

    n_pairs = (n_tiles - 1) // 2
    logits_tile(0, 0)
    lax.fori_loop(0, n_pairs, attend_pair, 0)
    t_rest = 2 * n_pairs

    @pl.when(t_rest + 1 < n_tiles)
    def _():
        logits_tile(t_rest + 1, 1)
        softmax_pv_tile(t_rest, 0)
        softmax_pv_tile(t_rest + 1, 1)

    @pl.when(t_rest + 1 >= n_tiles)
    def _():
        softmax_pv_tile(t_rest, 0)

    for h in range(A_HEADS):
        l_tot = jnp.sum(l_scr[h], axis=1, keepdims=True)
        o_ref[0, :, h * A_HEAD_DIM:(h + 1) * A_HEAD_DIM] = (acc_scr[h] / l_tot).astype(BF16)


def _dsa(main3, keys_src, ki, wi, *, k_col, v_col, q_off, n_keys, tq, ts):
    assert tq == A_HEAD_DIM, "the mask rides in the unused half of a 2*A_HEAD_DIM-deep contraction"
    b, t_q, _ = main3.shape
    s_len = keys_src.shape[1]
    topk = min(TOPK_MAX, n_keys // 4)
    dynamic = q_off == 0
    kern = functools.partial(_dsa_kernel, tq=tq, ts=ts, q_off=q_off, n_keys=n_keys, topk=topk,
                             n_tiles_static=None if dynamic else s_len // ts)
    return pl.pallas_call(
        kern,
        grid=(b, t_q // tq),
        in_specs=[pl.BlockSpec((1, tq, A_WIDTH), lambda i, j: (i, j, COL_Q // A_WIDTH)),
                  pl.BlockSpec((1, tq, IDX_HEADS * IDX_DIM), lambda i, j: (i, j, COL_IQ // (IDX_HEADS * IDX_DIM))),
                  pl.BlockSpec((1, IDX_HEADS, tq), lambda i, j: (i, 0, j)),
                  pl.BlockSpec((1, s_len, KV_WIDTH), lambda i, j: (i, 0, k_col)),
                  pl.BlockSpec((1, s_len, KV_WIDTH), lambda i, j: (i, 0, v_col)),
                  pl.BlockSpec((1, s_len, IDX_DIM), lambda i, j: (i, 0, 0))],
        out_specs=pl.BlockSpec((1, tq, A_WIDTH), lambda i, j: (i, j, 0)),
        out_shape=jax.ShapeDtypeStruct((b, t_q, A_WIDTH), BF16),
        scratch_shapes=[pltpu.VMEM((s_len // ts, ts, tq), jnp.int32),
                        pltpu.VMEM((s_len // ts + 1, ts // SLAB, 32, 8, tq), jnp.int32),
                        pltpu.VMEM((s_len // ts + 1, ts // SLAB, 8, tq), jnp.int32),
                        pltpu.VMEM((A_KV_HEADS, A_GROUP * tq, A_HEAD_DIM + tq), BF16),
                        pltpu.VMEM((IDX_HEADS // 2, 2 * tq, IDX_DIM), BF16),
                        pltpu.VMEM((2, A_KV_HEADS, A_GROUP * tq, ts), F32),
                        pltpu.VMEM((A_HEADS, tq, LANES), F32),
                        pltpu.VMEM((A_HEADS, tq, LANES), F32),
                        pltpu.VMEM((A_HEADS, tq, A_HEAD_DIM), F32)],
        compiler_params=_cparams(("arbitrary", "arbitrary")),
        name="dsa",
    )(main3, main3, wi, keys_src, keys_src, ki)


def _log_sigmoid(x):
    return jnp.minimum(x, 0.0) - jnp.log1p(jnp.exp(-jnp.abs(x)))


def _mlstm_kernel(q_ref, k_ref, v_ref, o_ref, g_ref, gb_ref, ng_ref, c0_ref, n0_ref, m0_ref,
                  y_ref, c_out, n_out, m_out, c_scr, n_scr, m_scr, *, chunk, valid_len, batches):
    step = pl.program_id(1)

    @pl.when(step == 0)
    def _():
        c_scr[...] = c0_ref[...]
        n_scr[...] = n0_ref[...]
        m_scr[...] = m0_ref[...]

    for bi in range(batches):
        _mlstm_chunk(bi, q_ref, k_ref, v_ref, o_ref, g_ref, gb_ref, ng_ref, y_ref, c_scr, n_scr, m_scr,
                     chunk=chunk, valid_len=valid_len)

    c_out[...] = c_scr[...]
    n_out[...] = n_scr[...]
    m_out[...] = m_scr[...]


def _mlstm_chunk(bi, q_ref, k_ref, v_ref, o_ref, g_ref, gb_ref, ng_ref, y_ref, c_scr, n_scr, m_scr,
                 *, chunk, valid_len):
    L = chunk
    g = g_ref[bi] + gb_ref[...]
    lf = _log_sigmoid(g)
    if valid_len < L:
        live = lax.broadcasted_iota(jnp.int32, (L, 1), 0) < valid_len
        lf = jnp.where(live, lf, 0.0)
        g = jnp.where(live, g, NEG_INF)
    g_t = g.T
    lf_t = lf.T
    row = lax.broadcasted_iota(jnp.int32, (L, L), 0)
    col = lax.broadcasted_iota(jnp.int32, (L, L), 1)
    lower = row >= col
    b_c = jnp.dot(jnp.where(lower, 1.0, 0.0), lf, precision=lax.Precision.HIGHEST,
                  preferred_element_type=F32)
    b_r = jnp.dot(lf_t, jnp.where(lower, 0.0, 1.0) + jnp.where(row == col, 1.0, 0.0),
                  precision=lax.Precision.HIGHEST, preferred_element_type=F32)

    for h in range(M_HEADS):
        sl = slice(h * M_DIM, (h + 1) * M_DIM)
        b_col = b_c[:, SM_MF + h:SM_MF + h + 1]
        ig_col = g[:, SM_MI + h:SM_MI + h + 1]
        a_row = g_t[SM_MI + h:SM_MI + h + 1, :] - b_r[SM_MF + h:SM_MF + h + 1, :]
        m_prev = m_scr[bi, h, 0:1, 0:1]
        d_mat = jnp.where(lower, b_col + a_row, NEG_INF)
        inter = b_col + m_prev
        m_t = jnp.maximum(inter, jnp.max(d_mat, axis=1, keepdims=True))
        w_intra = jnp.exp(d_mat - m_t)
        w_inter = jnp.exp(inter - m_t)
        qh = q_ref[bi, :, sl]
        kf = k_ref[bi, :, sl].astype(F32) * M_DIM ** -0.5
        vh = v_ref[bi, :, sl]
        s = _dot_nt(qh, kf.astype(BF16)) * w_intra
        c_old = c_scr[bi, h]
        n_old = n_scr[bi, h, 0:1, :]
        num = w_inter * _dot(qh, c_old.astype(BF16)) + _dot(s.astype(BF16), vh)
        den = (w_inter * jnp.sum(qh.astype(F32) * n_old, axis=1, keepdims=True)
               + jnp.sum(s, axis=1, keepdims=True))
        hid = num / jnp.maximum(jnp.abs(den), jnp.exp(-m_t))
        m_end = m_t[L - 1:L, :]
        b_last = b_col[L - 1:L, :]
        w_end = jnp.exp(b_last - b_col + ig_col - m_end)
        decay = jnp.exp(b_last + m_prev - m_end)
        kw = w_end * kf
        c_scr[bi, h] = decay * c_old + _dot(kw.T.astype(BF16), vh)
        n_scr[bi, h] = jnp.broadcast_to(decay * n_old + jnp.sum(kw, axis=0, keepdims=True), (8, M_DIM))
        m_scr[bi, h] = jnp.broadcast_to(m_end, (8, LANES))
        gated = _sigmoid(o_ref[bi, :, sl].astype(F32)) * hid
        y_ref[bi, :, sl] = (_normalize(gated) * ng_ref[:, sl]).astype(BF16)


def _mlstm(main3, small3, gate_bias, norm_g, c0, n0, m0, *, chunk, valid_len, batches):
    b, t, _ = main3.shape
    bb = batches
    blk = lambda col: pl.BlockSpec((bb, chunk, M_WIDTH), lambda i, j: (i, j, col // M_WIDTH))
    st = lambda shape: pl.BlockSpec((bb,) + shape, lambda i, j: (i,) + (0,) * len(shape))
    c_shape, n_shape, m_shape = (M_HEADS, M_DIM, M_DIM), (M_HEADS, 8, M_DIM), (M_HEADS, 8, LANES)
    return pl.pallas_call(
        functools.partial(_mlstm_kernel, chunk=chunk, valid_len=valid_len, batches=bb),
        grid=(b // bb, t // chunk),
        in_specs=[blk(COL_MQ), blk(COL_MK), blk(COL_MV), blk(COL_MO),
                  pl.BlockSpec((bb, chunk, LANES), lambda i, j: (i, j, 0)),
                  pl.BlockSpec((1, LANES), lambda i, j: (0, 0)),
                  pl.BlockSpec((1, M_WIDTH), lambda i, j: (0, 0)),
                  st(c_shape), st(n_shape), st(m_shape)],
        out_specs=[pl.BlockSpec((bb, chunk, M_WIDTH), lambda i, j: (i, j, 0)),
                   st(c_shape), st(n_shape), st(m_shape)],
        out_shape=[jax.ShapeDtypeStruct((b, t, M_WIDTH), BF16),
                   jax.ShapeDtypeStruct((b,) + c_shape, F32),
                   jax.ShapeDtypeStruct((b,) + n_shape, F32),
                   jax.ShapeDtypeStruct((b,) + m_shape, F32)],
        scratch_shapes=[pltpu.VMEM((bb,) + c_shape, F32), pltpu.VMEM((bb,) + n_shape, F32),
                        pltpu.VMEM((bb,) + m_shape, F32)],
        compiler_params=_cparams(("arbitrary", "arbitrary")),
        name="mlstm",
    )(main3, main3, main3, main3, small3, gate_bias, norm_g.reshape(1, M_WIDTH), c0, n0, m0)


def _conv_kernel(cu_ref, halo_ref, st_ref, w_ref, cb_ref, ng_ref, nb_ref, y_ref, ns_ref, full_scr, sh_scr,
                 *, tm, valid_len):
    i = pl.program_id(1)

    def glu(u):
        return u[:, :CONV_CH].astype(F32) * _sigmoid(u[:, CONV_CH:].astype(F32))

    @pl.when(i == 0)
    def _():
        full_scr[0:CONV_HALO, :] = st_ref[0]

    @pl.when(i > 0)
    def _():
        full_scr[0:CONV_HALO, :] = glu(halo_ref[0])

    full_scr[CONV_HALO:CONV_HALO + tm, :] = glu(cu_ref[0])
    span = tm + CONV_HALO - 8
    for b in range(1, 8):
        sh_scr[b - 1] = full_scr[pl.ds(b, span), :]
    for c0 in range(0, tm, CONV_ROWS):
        acc = None
        for j in range(CONV_W):
            a8, b = (CONV_PAD + j) // 8 * 8, (CONV_PAD + j) % 8
            if b == 0:
                src = full_scr[pl.ds(c0 + a8, CONV_ROWS), :]
            else:
                src = sh_scr[b - 1, pl.ds(c0 + a8, CONV_ROWS), :]
            term = w_ref[j:j + 1, :] * src
            acc = term if acc is None else acc + term
        y = _normalize(acc + cb_ref[...]) * ng_ref[...] + nb_ref[...]
        y_ref[0, c0:c0 + CONV_ROWS, :] = (y * _sigmoid(y)).astype(BF16)
    ns_ref[0] = full_scr[pl.ds(valid_len, CONV_HALO), :]


def _conv(main3, state, w, bias, ng, nb, *, tm, valid_len):
    b, t, _ = main3.shape
    cu_blk = COL_CU // (2 * CONV_CH)
    hpb = tm // CONV_HALO
    vec = pl.BlockSpec((1, CONV_CH), lambda i, j: (0, 0))
    return pl.pallas_call(
        functools.partial(_conv_kernel, tm=tm, valid_len=valid_len),
        grid=(b, t // tm),
        in_specs=[pl.BlockSpec((1, tm, 2 * CONV_CH), lambda i, j: (i, j, cu_blk)),
                  pl.BlockSpec((1, CONV_HALO, 2 * CONV_CH), lambda i, j: (i, jnp.maximum(j * hpb - 1, 0), cu_blk)),
                  pl.BlockSpec((1, CONV_HALO, CONV_CH), lambda i, j: (i, 0, 0)),
                  pl.BlockSpec((CONV_W, CONV_CH), lambda i, j: (0, 0)), vec, vec, vec],
        out_specs=[pl.BlockSpec((1, tm, CONV_CH), lambda i, j: (i, j, 0)),
                   pl.BlockSpec((1, CONV_HALO, CONV_CH), lambda i, j: (i, 0, 0))],
        out_shape=[jax.ShapeDtypeStruct((b, t, CONV_CH), BF16),
                   jax.ShapeDtypeStruct((b, CONV_HALO, CONV_CH), F32)],
        scratch_shapes=[pltpu.VMEM((CONV_HALO + tm, CONV_CH), F32),
                        pltpu.VMEM((7, CONV_HALO + tm - 8, CONV_CH), F32)],
        compiler_params=_cparams(("arbitrary", "arbitrary")),
        name="conv",
    )(main3, main3, state, w, bias.reshape(1, CONV_CH), ng.reshape(1, CONV_CH), nb.reshape(1, CONV_CH))


def _split_w_in(w_in):
    sizes = (A_WIDTH, KV_WIDTH, KV_WIDTH, IDX_HEADS * IDX_DIM, IDX_DIM, IDX_HEADS,
             M_WIDTH, M_WIDTH, M_WIDTH, M_HEADS, M_HEADS, M_WIDTH, 2 * CONV_CH)
    parts, start = [], 0
    for size in sizes:
        parts.append(w_in[:, start:start + size])
        start += size
    aq, ak, av, iq, ik, iw, mq, mk, mv, mi, mf, mo, cu = parts
    aq = aq * ATTN_LOG2E_SCALE
    w_main = jnp.concatenate([aq, ak, av, iq, mq, mk, mv, mo, cu], axis=1).astype(BF16)
    pad = jnp.zeros((w_in.shape[0], LANES - SM_USED), w_in.dtype)
    w_small = jnp.concatenate([ik, iw, mi, mf, pad], axis=1).astype(BF16)
    return w_main, w_small


def _tiles(m, rows_per_batch, per_row):
    if per_row:
        return dict(inproj=m, outproj=m, ffn=m, dsa_q=128, dsa_k=512, mlstm=128, mlstm_batches=1, conv=128)
    t = rows_per_batch
    return dict(inproj=min(1024, t), outproj=min(512, t), ffn=min(1024, t),
                dsa_q=min(128, t), dsa_k=min(512, t), mlstm=min(256, t), mlstm_batches=1, conv=min(512, t))


def _layer(x, ada, p, cache, *, rows_per_batch, alpha):
    m, d = x.shape
    nb = m // rows_per_batch
    t = rows_per_batch
    per_row = cache is not None
    tiles = _tiles(m, t, per_row)
    if per_row:
        mods = [jnp.repeat(v, t, axis=0)[None] for v in jnp.split(ada, 6, axis=-1)]
    else:
        mods = [v[:, None, :] for v in jnp.split(ada, 6, axis=-1)]
    sh1, sc1, g1, sh2, sc2, g2 = mods

    main, k_rows, v_rows, small = _inproj(x, sh1, sc1, p['w_main'], p['w_small'], per_row, t, tiles['inproj'])
    k_new = k_rows.reshape(nb, t, A_KV_HEADS, A_HEAD_DIM)
    v_new = v_rows.reshape(nb, t, A_KV_HEADS, A_HEAD_DIM)
    ik_new = small[:, SM_IK:SM_IK + IDX_DIM].reshape(nb, t, IDX_DIM)

    gate_bias = jnp.zeros((1, LANES), F32)
    gate_bias = gate_bias.at[0, SM_MI:SM_MI + M_HEADS].set(p['b_igate'])
    gate_bias = gate_bias.at[0, SM_MF:SM_MF + M_HEADS].set(p['b_fgate'])

    if cache is None:
        main3 = main.reshape(nb, t, MAIN_COLS)
        small3 = small.reshape(nb, t, LANES)
        ki = ik_new.astype(BF16)
        wi = jnp.swapaxes(small3[:, :, SM_IW:SM_IW + IDX_HEADS], 1, 2)
        attn = _dsa(main3, main3, ki, wi, k_col=COL_K // KV_WIDTH, v_col=COL_V // KV_WIDTH,
                    q_off=0, n_keys=t, tq=tiles['dsa_q'], ts=tiles['dsa_k'])
        c0 = jnp.zeros((nb, M_HEADS, M_DIM, M_DIM), F32)
        n0 = jnp.zeros((nb, M_HEADS, 8, M_DIM), F32)
        m0 = jnp.zeros((nb, M_HEADS, 8, LANES), F32)
        mls, c_new, n_new, m_new = _mlstm(main3, small3, gate_bias, p['mlstm_norm_g'], c0, n0, m0,
                                          chunk=tiles['mlstm'], valid_len=tiles['mlstm'],
                                          batches=tiles['mlstm_batches'])
        state = jnp.zeros((nb, CONV_HALO, CONV_CH), F32)
        cnv, conv_new = _conv(main3, state, p['conv_w'], p['conv_b'], p['conv_norm_g'], p['conv_norm_b'],
                              tm=tiles['conv'], valid_len=tiles['conv'])
        attn, mls, cnv = (a.reshape(m, -1) for a in (attn, mls, cnv))
    else:
        ck, cv, cik, c_st, n_st, m_st, conv_st = cache
        past = ck.shape[1]
        n_keys = past + t
        tp = tiles['dsa_q']
        ts = tiles['dsa_k']
        s_pad = -(-n_keys // ts) * ts
        main3 = jnp.pad(main.reshape(nb, t, MAIN_COLS), ((0, 0), (0, tp - t), (0, 0)))
        small3 = jnp.pad(small.reshape(nb, t, LANES), ((0, 0), (0, tp - t), (0, 0)))
        pad_keys = lambda a: jnp.pad(a, ((0, 0), (0, s_pad - n_keys), (0, 0)))
        k_all = pad_keys(jnp.concatenate([ck.reshape(nb, past, KV_WIDTH), k_rows.reshape(nb, t, KV_WIDTH)], axis=1))
        v_all = pad_keys(jnp.concatenate([cv.reshape(nb, past, KV_WIDTH), v_rows.reshape(nb, t, KV_WIDTH)], axis=1))
        kv_all = jnp.concatenate([k_all, v_all], axis=-1).astype(BF16)
        ki = pad_keys(jnp.concatenate([cik, ik_new], axis=1)).astype(BF16)
        wi = jnp.swapaxes(small3[:, :, SM_IW:SM_IW + IDX_HEADS], 1, 2)
        attn = _dsa(main3, kv_all, ki, wi, k_col=0, v_col=1, q_off=past, n_keys=n_keys, tq=tp, ts=ts)
        n0 = jnp.broadcast_to(n_st[:, :, None, :], (nb, M_HEADS, 8, M_DIM))
        m0 = jnp.broadcast_to(m_st[:, :, None, None], (nb, M_HEADS, 8, LANES))
        mls, c_new, n_new, m_new = _mlstm(main3, small3, gate_bias, p['mlstm_norm_g'], c_st, n0, m0,
                                          chunk=tp, valid_len=t, batches=tiles['mlstm_batches'])
        state = jnp.pad(conv_st, ((0, 0), (CONV_PAD, 0), (0, 0)))
        cnv, conv_new = _conv(main3, state, p['conv_w'], p['conv_b'], p['conv_norm_g'], p['conv_norm_b'],
                              tm=tp, valid_len=t)
        attn, mls, cnv = (a[:, :t].reshape(m, -1) for a in (attn, mls, cnv))

    x = _outproj(attn, mls, cnv, x, g1, p['w_out'], p['ln1_g'], p['ln1_b'], per_row, t, tiles['outproj'], alpha)
    x = _ffn(x, sh2, sc2, g2, p['w_ff1'], p['w_ff2'], p['ln2_g'], p['ln2_b'], per_row, t, tiles['ffn'], alpha)
    new_state = (k_new, v_new, ik_new, c_new, n_new[:, :, 0, :], m_new[:, :, 0, 0], conv_new[:, CONV_PAD:, :])
    return x, new_state


def kernel(x_prompt, x_sample, c_prompt, c_sample, cache_attn_k, cache_attn_v, cache_idx_k,
           state_mlstm_C, state_mlstm_n, state_mlstm_m, state_conv,
           w_ada, b_ada, w_in, b_igate, b_fgate, mlstm_norm_g, conv_w, conv_b,
           conv_norm_g, conv_norm_b, w_out, ln1_g, ln1_b, w_ff1, w_ff2, ln2_g, ln2_b):
    depth = w_ada.shape[0]
    alpha = (2 * depth) ** 0.25
    bp, tp, d = x_prompt.shape
    bs, tsmp, _ = x_sample.shape

    c_all = jnp.concatenate([c_prompt, c_sample], axis=0)
    rows = -(-c_all.shape[0] // 8) * 8
    ada_all = _ada(jnp.pad(c_all, ((0, rows - c_all.shape[0]), (0, 0))), w_ada, b_ada)

    xp = x_prompt.reshape(bp * tp, d)
    xs = x_sample.reshape(bs * tsmp, d)
    st_p, st_s = [], []
    for l in range(depth):
        w_main, w_small = _split_w_in(w_in[l])
        p = {'w_main': w_main, 'w_small': w_small, 'b_igate': b_igate[l], 'b_fgate': b_fgate[l],
             'mlstm_norm_g': mlstm_norm_g[l], 'conv_w': conv_w[l], 'conv_b': conv_b[l],
             'conv_norm_g': conv_norm_g[l], 'conv_norm_b': conv_norm_b[l],
             'w_out': w_out[l].astype(BF16), 'ln1_g': ln1_g[l], 'ln1_b': ln1_b[l],
             'w_ff1': w_ff1[l].astype(BF16), 'w_ff2': w_ff2[l].astype(BF16),
             'ln2_g': ln2_g[l], 'ln2_b': ln2_b[l]}
        xp, sp = _layer(xp, ada_all[l, :bp], p, None, rows_per_batch=tp, alpha=alpha)
        cache = (cache_attn_k[l], cache_attn_v[l], cache_idx_k[l], state_mlstm_C[l],
                 state_mlstm_n[l], state_mlstm_m[l], state_conv[l])
        xs, ss = _layer(xs, ada_all[l, bp:bp + bs], p, cache, rows_per_batch=tsmp, alpha=alpha)
        st_p.append(sp)
        st_s.append(ss)

    def stack(sts, i):
        return jnp.stack([s[i] for s in sts], axis=0)

    return (xp.reshape(bp, tp, d), xs.reshape(bs, tsmp, d),
            *(stack(st_p, i) for i in range(7)), *(stack(st_s, i) for i in range(7)))
```

```python
import functools

import jax
import jax.numpy as jnp
from jax import lax
from jax.experimental import pallas as pl
from jax.experimental.pallas import tpu as pltpu

F32 = jnp.float32
BF16 = jnp.bfloat16

CHUNK = 64
A_HEADS = 8
A_KV_HEADS = 2
A_GROUP = A_HEADS // A_KV_HEADS
A_HEAD_DIM = 128
A_WIDTH = A_HEADS * A_HEAD_DIM
KV_WIDTH = A_KV_HEADS * A_HEAD_DIM
IDX_HEADS = 8
IDX_DIM = 64
TOPK_MAX = 256
M_HEADS = 4
M_DIM = 128
M_WIDTH = M_HEADS * M_DIM
CONV_CH = 512
CONV_W = 31
LN_EPS = 1e-5
NEG_INF = -1e30
ATTN_LOG2E_SCALE = A_HEAD_DIM ** -0.5 * 1.4426950408889634
INT_MIN = -2 ** 31

LANES = 128
VMEM_LIMIT_BYTES = 56 * 1024 * 1024

COL_Q = 0
COL_K = COL_Q + A_WIDTH
COL_V = COL_K + KV_WIDTH
COL_IQ = COL_V + KV_WIDTH
COL_MQ = COL_IQ + IDX_HEADS * IDX_DIM
COL_MK = COL_MQ + M_WIDTH
COL_MV = COL_MK + M_WIDTH
COL_MO = COL_MV + M_WIDTH
COL_CU = COL_MO + M_WIDTH
MAIN_COLS = COL_CU + 2 * CONV_CH
SM_IK = 0
SM_IW = SM_IK + IDX_DIM
SM_MI = SM_IW + IDX_HEADS
SM_MF = SM_MI + M_HEADS
SM_USED = SM_MF + M_HEADS

SLAB = 32 * 8

CONV_HALO = 32
CONV_PAD = CONV_HALO - (CONV_W - 1)
CONV_ROWS = 32


def _cparams(sem):
    return pltpu.CompilerParams(dimension_semantics=sem, vmem_limit_bytes=VMEM_LIMIT_BYTES)


def _normalize(x):
    mu = jnp.mean(x, axis=-1, keepdims=True)
    xc = x - mu
    var = jnp.mean(xc * xc, axis=-1, keepdims=True)
    return xc * lax.rsqrt(var + LN_EPS)


def _sigmoid(x):
    return 1.0 / (1.0 + jnp.exp(-x))


def _dot(a, b):
    return jnp.dot(a, b, preferred_element_type=F32)


def _dot_nt(a, b):
    return lax.dot_general(a, b, (((1,), (1,)), ((), ())), preferred_element_type=F32)


def _ada_kernel(c_ref, w_ref, b_ref, o_ref):
    c = c_ref[...]
    s = (c * _sigmoid(c)).astype(BF16)
    o_ref[0] = _dot(s, w_ref[0].astype(BF16)) + b_ref[0]


def _ada(c_all, w_ada, b_ada):
    depth, d, n = w_ada.shape
    rows = c_all.shape[0]
    tn = 1024
    return pl.pallas_call(
        _ada_kernel,
        grid=(depth, n // tn),
        in_specs=[pl.BlockSpec((rows, d), lambda l, j: (0, 0)),
                  pl.BlockSpec((1, d, tn), lambda l, j: (l, 0, j)),
                  pl.BlockSpec((1, 1, tn), lambda l, j: (l, 0, j))],
        out_specs=pl.BlockSpec((1, rows, tn), lambda l, j: (l, 0, j)),
        out_shape=jax.ShapeDtypeStruct((depth, rows, n), F32),
        compiler_params=_cparams(("arbitrary", "arbitrary")),
        name="ada",
    )(c_all, w_ada, b_ada.reshape(depth, 1, n))


def _mod_spec(per_row, tm, rows_per_batch, d):
    if per_row:
        return pl.BlockSpec((1, tm, d), lambda i, *_: (0, i, 0))
    bpb = rows_per_batch // tm
    return pl.BlockSpec((1, 1, d), lambda i, *_: (i // bpb, 0, 0))


def _inproj_kernel(x_ref, sh_ref, sc_ref, wm_ref, ws_ref, main_ref, k_ref, v_ref, small_ref, h_scr, *, kv_block):
    j = pl.program_id(1)

    @pl.when(j == 0)
    def _():
        h = _normalize(x_ref[...]) * (1.0 + sc_ref[0]) + sh_ref[0]
        hb = h.astype(BF16)
        h_scr[...] = hb
        small_ref[...] = _dot(hb, ws_ref[...])

    acc = _dot(h_scr[...], wm_ref[...])
    main_ref[...] = acc.astype(BF16)

    @pl.when(j == kv_block)
    def _():
        k_ref[...] = acc[:, :KV_WIDTH]
        v_ref[...] = acc[:, KV_WIDTH:]


def _inproj(x, sh, sc, w_main, w_small, layer, per_row, rows_per_batch, tm):
    m, d = x.shape
    tn = 2 * KV_WIDTH
    mod = _mod_spec(per_row, tm, rows_per_batch, d)
    return pl.pallas_call(
        functools.partial(_inproj_kernel, kv_block=COL_K // tn),
        grid=(m // tm, MAIN_COLS // tn),
        in_specs=[pl.BlockSpec((tm, d), lambda i, j: (i, 0)), mod, mod,
                  pl.BlockSpec((None, d, tn), lambda i, j: (layer, 0, j)),
                  pl.BlockSpec((None, d, LANES), lambda i, j: (layer, 0, 0))],
        out_specs=[pl.BlockSpec((tm, tn), lambda i, j: (i, j)),
                   pl.BlockSpec((tm, KV_WIDTH), lambda i, j: (i, 0)),
                   pl.BlockSpec((tm, KV_WIDTH), lambda i, j: (i, 0)),
                   pl.BlockSpec((tm, LANES), lambda i, j: (i, 0))],
        out_shape=[jax.ShapeDtypeStruct((m, MAIN_COLS), BF16),
                   jax.ShapeDtypeStruct((m, KV_WIDTH), F32),
                   jax.ShapeDtypeStruct((m, KV_WIDTH), F32),
                   jax.ShapeDtypeStruct((m, LANES), F32)],
        scratch_shapes=[pltpu.VMEM((tm, d), BF16)],
        compiler_params=_cparams(("arbitrary", "arbitrary")),
        name="inproj",
    )(x, sh, sc, w_main, w_small)


def _outproj_kernel(a_ref, m_ref, c_ref, x_ref, g_ref, w_ref, lg_ref, lb_ref, o_ref, *, alpha):
    y = _dot(a_ref[...], w_ref[0:A_WIDTH, :])
    y += _dot(m_ref[...], w_ref[A_WIDTH:A_WIDTH + M_WIDTH, :])
    y += _dot(c_ref[...], w_ref[A_WIDTH + M_WIDTH:, :])
    z = alpha * x_ref[...] + (1.0 + g_ref[0]) * y
    o_ref[...] = _normalize(z) * lg_ref[...] + lb_ref[...]


def _outproj(attn, mls, cnv, x, gate, w_out, layer, ln_g, ln_b, per_row, rows_per_batch, tm, alpha):
    m, d = x.shape
    row = lambda w: pl.BlockSpec((tm, w), lambda i: (i, 0))
    vec = pl.BlockSpec((1, d), lambda i: (0, 0))
    return pl.pallas_call(
        functools.partial(_outproj_kernel, alpha=alpha),
        grid=(m // tm,),
        in_specs=[row(A_WIDTH), row(M_WIDTH), row(CONV_CH), row(d),
                  _mod_spec(per_row, tm, rows_per_batch, d),
                  pl.BlockSpec((None,) + w_out.shape[1:], lambda i: (layer, 0, 0)), vec, vec],
        out_specs=row(d),
        out_shape=jax.ShapeDtypeStruct((m, d), F32),
        compiler_params=_cparams(("arbitrary",)),
        name="outproj",
    )(attn, mls, cnv, x, gate, w_out, ln_g.reshape(1, d), ln_b.reshape(1, d))


def _ffn_kernel(x_ref, sh_ref, sc_ref, g_ref, w1_ref, w2_ref, lg_ref, lb_ref, o_ref, h_scr, *, alpha):
    j = pl.program_id(1)

    @pl.when(j == 0)
    def _():
        h = _normalize(x_ref[...]) * (1.0 + sc_ref[0]) + sh_ref[0]
        h_scr[...] = h.astype(BF16)
        o_ref[...] = jnp.zeros_like(o_ref)

    u = jnp.maximum(_dot(h_scr[...], w1_ref[...]), 0.0)
    o_ref[...] += _dot((u * u).astype(BF16), w2_ref[...])

    @pl.when(j == pl.num_programs(1) - 1)
    def _():
        z = alpha * x_ref[...] + (1.0 + g_ref[0]) * o_ref[...]
        o_ref[...] = _normalize(z) * lg_ref[...] + lb_ref[...]


def _ffn(x, sh, sc, gate, w1, w2, layer, ln_g, ln_b, per_row, rows_per_batch, tm, alpha):
    m, d = x.shape
    dff = w1.shape[2]
    tf = 512
    mod = _mod_spec(per_row, tm, rows_per_batch, d)
    vec = pl.BlockSpec((1, d), lambda i, j: (0, 0))
    return pl.pallas_call(
        functools.partial(_ffn_kernel, alpha=alpha),
        grid=(m // tm, dff // tf),
        in_specs=[pl.BlockSpec((tm, d), lambda i, j: (i, 0)), mod, mod, mod,
                  pl.BlockSpec((None, d, tf), lambda i, j: (layer, 0, j)),
                  pl.BlockSpec((None, tf, d), lambda i, j: (layer, j, 0)), vec, vec],
        out_specs=pl.BlockSpec((tm, d), lambda i, j: (i, 0)),
        out_shape=jax.ShapeDtypeStruct((m, d), F32),
        scratch_shapes=[pltpu.VMEM((tm, d), BF16)],
        compiler_params=_cparams(("arbitrary", "arbitrary")),
        name="ffn",
    )(x, sh, sc, gate, w1, w2, ln_g.reshape(1, d), ln_b.reshape(1, d))


def _dsa_kernel(q_ref, qi_ref, wi_ref, k_ref, v_ref, ki_ref, o_ref,
                key_scr, plane_scr, alive_scr, qs_scr, qis_scr, s_scr, m_scr, l_scr, acc_scr,
                *, tq, ts, q_off, n_keys, topk, n_tiles_static):
    qb = pl.program_id(1)
    q_pos0 = q_off + qb * tq
    if n_tiles_static is None:
        vis_max = jnp.minimum(((q_pos0 + tq - 1) // CHUNK + 1) * CHUNK, n_keys)
        n_tiles = (vis_max + ts - 1) // ts
    else:
        n_tiles = n_tiles_static
    lane_tiles = [slice(c * LANES, (c + 1) * LANES) for c in range(ts // LANES)]
    n_slabs = ts // SLAB

    q_chunk = (q_pos0 + lax.broadcasted_iota(jnp.int32, (1, tq), 1)) // CHUNK
    n_vis = jnp.minimum((q_chunk + 1) * CHUNK, n_keys)
    kk = jnp.minimum(topk, n_vis).astype(F32)

    eye = (lax.broadcasted_iota(jnp.int32, (tq, tq), 0)
           == lax.broadcasted_iota(jnp.int32, (tq, tq), 1)).astype(BF16)
    for h in range(A_HEADS):
        rows = slice((h % A_GROUP) * tq, (h % A_GROUP + 1) * tq)
        qs_scr[h // A_GROUP, rows, 0:A_HEAD_DIM] = q_ref[0, :, h * A_HEAD_DIM:(h + 1) * A_HEAD_DIM]
        qs_scr[h // A_GROUP, rows, A_HEAD_DIM:A_HEAD_DIM + tq] = eye
    for h in range(IDX_HEADS):
        qis_scr[h // 2, (h % 2) * tq:(h % 2 + 1) * tq, :] = qi_ref[0, :, h * IDX_DIM:(h + 1) * IDX_DIM]
    wi = wi_ref[0] * IDX_HEADS ** -0.5 * IDX_DIM ** -0.5

    def score_tile(t):
        start = pl.multiple_of(t * ts, ts)
        kt = ki_ref[0, pl.ds(start, ts), :]
        score = None
        for pair in range(IDX_HEADS // 2):
            d = _dot_nt(kt, qis_scr[pair])
            for e in range(2):
                h = 2 * pair + e
                term = wi[h:h + 1, :] * jnp.maximum(d[:, e * tq:(e + 1) * tq], 0.0)
                score = term if score is None else score + term
        score = jnp.where(score == 0.0, 0.0, score)
        bits = lax.bitcast_convert_type(score, jnp.int32)
        key = bits ^ ((bits >> 31) & 0x7FFFFFFF)
        visible = lax.broadcasted_iota(jnp.int32, (ts, tq), 0) < n_vis - start
        key = jnp.where(visible, key, INT_MIN)
        key_scr[t] = key
        for sl in range(n_slabs):
            a = [key[sl * SLAB + 8 * j:sl * SLAB + 8 * j + 8, :] ^ INT_MIN for j in range(32)]
            dist, mask = 16, 0x0000FFFF
            while dist:
                k = 0
                while k < 32:
                    swap = (a[k] ^ lax.shift_right_logical(a[k + dist], dist)) & mask
                    a[k] = a[k] ^ swap
                    a[k + dist] = a[k + dist] ^ (swap << dist)
                    k = (k + dist + 1) & ~dist
                dist >>= 1
                mask ^= (mask << dist) & 0xFFFFFFFF
            present = a[0]
            for i in range(32):
                plane_scr[t, sl, i] = a[i]
                present = present | a[i]
            alive_scr[t, sl] = present

    def score_pair(i, carry):
        score_tile(2 * i)
        score_tile(2 * i + 1)
        return carry

    lax.fori_loop(0, n_tiles // 2, score_pair, 0)

    @pl.when(n_tiles % 2 == 1)
    def _():
        score_tile(n_tiles - 1)

    @pl.when(n_tiles % 2 == 1)
    def _():
        plane_scr[n_tiles] = jnp.zeros(plane_scr.shape[1:], jnp.int32)
        alive_scr[n_tiles] = jnp.zeros(alive_scr.shape[1:], jnp.int32)

    def sweep(i, take_prev):
        def body(pair, accs):
            accs = list(accs)
            for e in range(2):
                t = 2 * pair + e
                for sl in range(n_slabs):
                    a = alive_scr[t, sl]
                    if take_prev is not None:
                        x = a & plane_scr[t, sl, i - 1]
                        a = jnp.where(take_prev, x, a ^ x)
                        alive_scr[t, sl] = a
                    accs[e] = accs[e] + lax.population_count(a & plane_scr[t, sl, i])
            return tuple(accs)
        zero = jnp.zeros((8, tq), jnp.int32)
        acc0, acc1 = lax.fori_loop(0, (n_tiles + 1) // 2, body, (zero, zero))
        return jnp.sum((acc0 + acc1).astype(F32), axis=0, keepdims=True)

    def decide(i, cnt, want, n_alive, t_u):
        take = cnt >= want
        want = jnp.where(take, want, want - cnt)
        n_alive = jnp.where(take, cnt, n_alive - cnt)
        t_u = t_u | jnp.where(take, lax.shift_left(jnp.int32(1), 31 - i), 0)
        return take.astype(jnp.int32), want, n_alive, t_u

    def radix_step(i, carry):
        take_prev, want, n_alive, t_u = carry
        cnt = sweep(i, jnp.broadcast_to(take_prev, (8, tq)) != 0)
        return decide(i, cnt, want, n_alive, t_u)

    carry = decide(0, sweep(0, None), kk, n_vis.astype(F32), jnp.zeros((1, tq), jnp.int32))
    _, keep, n_alive, t_u = lax.fori_loop(1, 32, radix_step, carry)
    thr = jnp.broadcast_to(t_u ^ INT_MIN, (8, tq))

    @pl.when(jnp.max(n_alive - keep) > 0.0)
    def _():
        lower = (lax.broadcasted_iota(jnp.int32, (ts, ts), 0)
                 >= lax.broadcasted_iota(jnp.int32, (ts, ts), 1)).astype(BF16)

        def tie_tile(t, seen):
            blk = key_scr[t]
            eq = blk == thr[0:1, :]
            rank = seen + _dot(lower, jnp.where(eq, 1.0, 0.0).astype(BF16))
            key_scr[t] = jnp.where(eq & (rank > keep), INT_MIN, blk)
            return rank[ts - 1:ts, :]

        lax.fori_loop(0, n_tiles, tie_tile, jnp.zeros((1, tq), F32))

    m_scr[...] = jnp.full(m_scr.shape, NEG_INF, F32)
    l_scr[...] = jnp.zeros(l_scr.shape, F32)
    acc_scr[...] = jnp.zeros(acc_scr.shape, F32)

    def logits_tile(t, slot):
        start = pl.multiple_of(t * ts, ts)
        bias = jnp.where(key_scr[t] >= thr[0:1, :], 0.0, NEG_INF).astype(BF16)
        for n in range(A_KV_HEADS):
            kt = k_ref[0, pl.ds(start, ts), n * A_HEAD_DIM:(n + 1) * A_HEAD_DIM]
            s_scr[slot, n] = _dot_nt(qs_scr[n], jnp.concatenate([kt, bias], axis=1))

    def softmax_pv_tile(t, slot):
        start = pl.multiple_of(t * ts, ts)
        for n in range(A_KV_HEADS):
            vt = v_ref[0, pl.ds(start, ts), n * A_HEAD_DIM:(n + 1) * A_HEAD_DIM]
            for g in range(A_GROUP):
                h = n * A_GROUP + g
                s = [s_scr[slot, n, g * tq:(g + 1) * tq, ls] for ls in lane_tiles]
                mx = s[0]
                for sc in s[1:]:
                    mx = jnp.maximum(mx, sc)
                m_old = m_scr[h]
                m_new = jnp.maximum(m_old, jnp.max(mx, axis=1, keepdims=True))
                alpha = jnp.exp2(m_old - m_new)
                p = [jnp.exp2(sc - m_new) for sc in s]
                psum = p[0]
                for pc in p[1:]:
                    psum = psum + pc
                l_scr[h] = alpha * l_scr[h] + psum
                pv = _dot(jnp.concatenate(p, axis=1).astype(BF16), vt)
                acc_scr[h] = alpha * acc_scr[h] + pv
                m_scr[h] = m_new

    def attend_pair(i, carry):
        t = 2 * i
        logits_tile(t + 1, 1)
        softmax_pv_tile(t, 0)
        logits_tile(t + 2, 0)
        softmax_pv_tile(t + 1, 1)
        return carry

    n_pairs = (n_tiles - 1) // 2
    logits_tile(0, 0)
    lax.fori_loop(0, n_pairs, attend_pair, 0)
    t_rest = 2 * n_pairs

    @pl.when(t_rest + 1 < n_tiles)
    def _():
        logits_tile(t_rest + 1, 1)
        softmax_pv_tile(t_rest, 0)
        softmax_pv_tile(t_rest + 1, 1)

    @pl.when(t_rest + 1 >= n_tiles)
    def _():
        softmax_pv_tile(t_rest, 0)

    for h in range(A_HEADS):
        l_tot = jnp.sum(l_scr[h], axis=1, keepdims=True)
        o_ref[0, :, h * A_HEAD_DIM:(h + 1) * A_HEAD_DIM] = (acc_scr[h] / l_tot).astype(BF16)


def _dsa(main3, keys_src, ki, wi, *, k_col, v_col, q_off, n_keys, tq, ts):
    assert tq == A_HEAD_DIM, "the mask rides in the unused half of a 2*A_HEAD_DIM-deep contraction"
    b, t_q, _ = main3.shape
    s_len = keys_src.shape[1]
    topk = min(TOPK_MAX, n_keys // 4)
    dynamic = q_off == 0
    kern = functools.partial(_dsa_kernel, tq=tq, ts=ts, q_off=q_off, n_keys=n_keys, topk=topk,
                             n_tiles_static=None if dynamic else s_len // ts)
    return pl.pallas_call(
        kern,
        grid=(b, t_q // tq),
        in_specs=[pl.BlockSpec((1, tq, A_WIDTH), lambda i, j: (i, j, COL_Q // A_WIDTH)),
                  pl.BlockSpec((1, tq, IDX_HEADS * IDX_DIM), lambda i, j: (i, j, COL_IQ // (IDX_HEADS * IDX_DIM))),
                  pl.BlockSpec((1, IDX_HEADS, tq), lambda i, j: (i, 0, j)),
                  pl.BlockSpec((1, s_len, KV_WIDTH), lambda i, j: (i, 0, k_col)),
                  pl.BlockSpec((1, s_len, KV_WIDTH), lambda i, j: (i, 0, v_col)),
                  pl.BlockSpec((1, s_len, IDX_DIM), lambda i, j: (i, 0, 0))],
        out_specs=pl.BlockSpec((1, tq, A_WIDTH), lambda i, j: (i, j, 0)),
        out_shape=jax.ShapeDtypeStruct((b, t_q, A_WIDTH), BF16),
        scratch_shapes=[pltpu.VMEM((s_len // ts, ts, tq), jnp.int32),
                        pltpu.VMEM((s_len // ts + 1, ts // SLAB, 32, 8, tq), jnp.int32),
                        pltpu.VMEM((s_len // ts + 1, ts // SLAB, 8, tq), jnp.int32),
                        pltpu.VMEM((A_KV_HEADS, A_GROUP * tq, A_HEAD_DIM + tq), BF16),
                        pltpu.VMEM((IDX_HEADS // 2, 2 * tq, IDX_DIM), BF16),
                        pltpu.VMEM((2, A_KV_HEADS, A_GROUP * tq, ts), F32),
                        pltpu.VMEM((A_HEADS, tq, LANES), F32),
                        pltpu.VMEM((A_HEADS, tq, LANES), F32),
                        pltpu.VMEM((A_HEADS, tq, A_HEAD_DIM), F32)],
        compiler_params=_cparams(("arbitrary", "arbitrary")),
        name="dsa",
    )(main3, main3, wi, keys_src, keys_src, ki)


def _log_sigmoid(x):
    return jnp.minimum(x, 0.0) - jnp.log1p(jnp.exp(-jnp.abs(x)))


def _mlstm_kernel(q_ref, k_ref, v_ref, o_ref, g_ref, gb_ref, ng_ref, c0_ref, n0_ref, m0_ref,
                  y_ref, c_out, n_out, m_out, c_scr, n_scr, m_scr, *, chunk, valid_len, batches):
    step = pl.program_id(1)

    @pl.when(step == 0)
    def _():
        c_scr[...] = c0_ref[...]
        n_scr[...] = n0_ref[...]
        m_scr[...] = m0_ref[...]

    for bi in range(batches):
        _mlstm_chunk(bi, q_ref, k_ref, v_ref, o_ref, g_ref, gb_ref, ng_ref, y_ref, c_scr, n_scr, m_scr,
                     chunk=chunk, valid_len=valid_len)

    c_out[...] = c_scr[...]
    n_out[...] = n_scr[...]
    m_out[...] = m_scr[...]


def _mlstm_chunk(bi, q_ref, k_ref, v_ref, o_ref, g_ref, gb_ref, ng_ref, y_ref, c_scr, n_scr, m_scr,
                 *, chunk, valid_len):
    L = chunk
    g = g_ref[bi] + gb_ref[...]
    lf = _log_sigmoid(g)
    if valid_len < L:
        live = lax.broadcasted_iota(jnp.int32, (L, 1), 0) < valid_len
        lf = jnp.where(live, lf, 0.0)
        g = jnp.where(live, g, NEG_INF)
    g_t = g.T
    lf_t = lf.T
    row = lax.broadcasted_iota(jnp.int32, (L, L), 0)
    col = lax.broadcasted_iota(jnp.int32, (L, L), 1)
    lower = row >= col
    b_c = jnp.dot(jnp.where(lower, 1.0, 0.0), lf, precision=lax.Precision.HIGHEST,
                  preferred_element_type=F32)
    b_r = jnp.dot(lf_t, jnp.where(lower, 0.0, 1.0) + jnp.where(row == col, 1.0, 0.0),
                  precision=lax.Precision.HIGHEST, preferred_element_type=F32)

    for h in range(M_HEADS):
        sl = slice(h * M_DIM, (h + 1) * M_DIM)
        b_col = b_c[:, SM_MF + h:SM_MF + h + 1]
        ig_col = g[:, SM_MI + h:SM_MI + h + 1]
        a_row = g_t[SM_MI + h:SM_MI + h + 1, :] - b_r[SM_MF + h:SM_MF + h + 1, :]
        m_prev = m_scr[bi, h, 0:1, 0:1]
        d_mat = jnp.where(lower, b_col + a_row, NEG_INF)
        inter = b_col + m_prev
        m_t = jnp.maximum(inter, jnp.max(d_mat, axis=1, keepdims=True))
        w_intra = jnp.exp(d_mat - m_t)
        w_inter = jnp.exp(inter - m_t)
        qh = q_ref[bi, :, sl]
        kf = k_ref[bi, :, sl].astype(F32) * M_DIM ** -0.5
        vh = v_ref[bi, :, sl]
        s = _dot_nt(qh, kf.astype(BF16)) * w_intra
        c_old = c_scr[bi, h]
        n_old = n_scr[bi, h, 0:1, :]
        num = w_inter * _dot(qh, c_old.astype(BF16)) + _dot(s.astype(BF16), vh)
        den = (w_inter * jnp.sum(qh.astype(F32) * n_old, axis=1, keepdims=True)
               + jnp.sum(s, axis=1, keepdims=True))
        hid = num / jnp.maximum(jnp.abs(den), jnp.exp(-m_t))
        m_end = m_t[L - 1:L, :]
        b_last = b_col[L - 1:L, :]
        w_end = jnp.exp(b_last - b_col + ig_col - m_end)
        decay = jnp.exp(b_last + m_prev - m_end)
        kw = w_end * kf
        c_scr[bi, h] = decay * c_old + _dot(kw.T.astype(BF16), vh)
        n_scr[bi, h] = jnp.broadcast_to(decay * n_old + jnp.sum(kw, axis=0, keepdims=True), (8, M_DIM))
        m_scr[bi, h] = jnp.broadcast_to(m_end, (8, LANES))
        gated = _sigmoid(o_ref[bi, :, sl].astype(F32)) * hid
        y_ref[bi, :, sl] = (_normalize(gated) * ng_ref[:, sl]).astype(BF16)


def _mlstm(main3, small3, gate_bias, norm_g, c0, n0, m0, *, chunk, valid_len, batches):
    b, t, _ = main3.shape
    bb = batches
    blk = lambda col: pl.BlockSpec((bb, chunk, M_WIDTH), lambda i, j: (i, j, col // M_WIDTH))
    st = lambda shape: pl.BlockSpec((bb,) + shape, lambda i, j: (i,) + (0,) * len(shape))
    c_shape, n_shape, m_shape = (M_HEADS, M_DIM, M_DIM), (M_HEADS, 8, M_DIM), (M_HEADS, 8, LANES)
    return pl.pallas_call(
        functools.partial(_mlstm_kernel, chunk=chunk, valid_len=valid_len, batches=bb),
        grid=(b // bb, t // chunk),
        in_specs=[blk(COL_MQ), blk(COL_MK), blk(COL_MV), blk(COL_MO),
                  pl.BlockSpec((bb, chunk, LANES), lambda i, j: (i, j, 0)),
                  pl.BlockSpec((1, LANES), lambda i, j: (0, 0)),
                  pl.BlockSpec((1, M_WIDTH), lambda i, j: (0, 0)),
                  st(c_shape), st(n_shape), st(m_shape)],
        out_specs=[pl.BlockSpec((bb, chunk, M_WIDTH), lambda i, j: (i, j, 0)),
                   st(c_shape), st(n_shape), st(m_shape)],
        out_shape=[jax.ShapeDtypeStruct((b, t, M_WIDTH), BF16),
                   jax.ShapeDtypeStruct((b,) + c_shape, F32),
                   jax.ShapeDtypeStruct((b,) + n_shape, F32),
                   jax.ShapeDtypeStruct((b,) + m_shape, F32)],
        scratch_shapes=[pltpu.VMEM((bb,) + c_shape, F32), pltpu.VMEM((bb,) + n_shape, F32),
                        pltpu.VMEM((bb,) + m_shape, F32)],
        compiler_params=_cparams(("arbitrary", "arbitrary")),
        name="mlstm",
    )(main3, main3, main3, main3, small3, gate_bias, norm_g.reshape(1, M_WIDTH), c0, n0, m0)


def _conv_kernel(cu_ref, halo_ref, st_ref, w_ref, cb_ref, ng_ref, nb_ref, y_ref, ns_ref, full_scr, sh_scr,
                 *, tm, valid_len):
    i = pl.program_id(1)

    def glu(u):
        return u[:, :CONV_CH].astype(F32) * _sigmoid(u[:, CONV_CH:].astype(F32))

    @pl.when(i == 0)
    def _():
        full_scr[0:CONV_HALO, :] = st_ref[0]

    @pl.when(i > 0)
    def _():
        full_scr[0:CONV_HALO, :] = glu(halo_ref[0])

    full_scr[CONV_HALO:CONV_HALO + tm, :] = glu(cu_ref[0])
    span = tm + CONV_HALO - 8
    for b in range(1, 8):
        sh_scr[b - 1] = full_scr[pl.ds(b, span), :]
    for c0 in range(0, tm, CONV_ROWS):
        acc = None
        for j in range(CONV_W):
            a8, b = (CONV_PAD + j) // 8 * 8, (CONV_PAD + j) % 8
            if b == 0:
                src = full_scr[pl.ds(c0 + a8, CONV_ROWS), :]
            else:
                src = sh_scr[b - 1, pl.ds(c0 + a8, CONV_ROWS), :]
            term = w_ref[j:j + 1, :] * src
            acc = term if acc is None else acc + term
        y = _normalize(acc + cb_ref[...]) * ng_ref[...] + nb_ref[...]
        y_ref[0, c0:c0 + CONV_ROWS, :] = (y * _sigmoid(y)).astype(BF16)
    ns_ref[0] = full_scr[pl.ds(valid_len, CONV_HALO), :]


def _conv(main3, state, w, bias, ng, nb, *, tm, valid_len):
    b, t, _ = main3.shape
    cu_blk = COL_CU // (2 * CONV_CH)
    hpb = tm // CONV_HALO
    vec = pl.BlockSpec((1, CONV_CH), lambda i, j: (0, 0))
    return pl.pallas_call(
        functools.partial(_conv_kernel, tm=tm, valid_len=valid_len),
        grid=(b, t // tm),
        in_specs=[pl.BlockSpec((1, tm, 2 * CONV_CH), lambda i, j: (i, j, cu_blk)),
                  pl.BlockSpec((1, CONV_HALO, 2 * CONV_CH), lambda i, j: (i, jnp.maximum(j * hpb - 1, 0), cu_blk)),
                  pl.BlockSpec((1, CONV_HALO, CONV_CH), lambda i, j: (i, 0, 0)),
                  pl.BlockSpec((CONV_W, CONV_CH), lambda i, j: (0, 0)), vec, vec, vec],
        out_specs=[pl.BlockSpec((1, tm, CONV_CH), lambda i, j: (i, j, 0)),
                   pl.BlockSpec((1, CONV_HALO, CONV_CH), lambda i, j: (i, 0, 0))],
        out_shape=[jax.ShapeDtypeStruct((b, t, CONV_CH), BF16),
                   jax.ShapeDtypeStruct((b, CONV_HALO, CONV_CH), F32)],
        scratch_shapes=[pltpu.VMEM((CONV_HALO + tm, CONV_CH), F32),
                        pltpu.VMEM((7, CONV_HALO + tm - 8, CONV_CH), F32)],
        compiler_params=_cparams(("arbitrary", "arbitrary")),
        name="conv",
    )(main3, main3, state, w, bias.reshape(1, CONV_CH), ng.reshape(1, CONV_CH), nb.reshape(1, CONV_CH))


def _split_w_in(w_in):
    sizes = (A_WIDTH, KV_WIDTH, KV_WIDTH, IDX_HEADS * IDX_DIM, IDX_DIM, IDX_HEADS,
             M_WIDTH, M_WIDTH, M_WIDTH, M_HEADS, M_HEADS, M_WIDTH, 2 * CONV_CH)
    offs = [0]
    for size in sizes:
        offs.append(offs[-1] + size)
    aq, ak, av, iq, ik, iw, mq, mk, mv, mi, mf, mo, cu, end = offs
    w_main = jnp.concatenate([w_in[..., aq:ik], w_in[..., mq:mi], w_in[..., mo:end]], axis=-1)
    col_scale = jnp.where(jnp.arange(MAIN_COLS) < A_WIDTH, ATTN_LOG2E_SCALE, 1.0).astype(w_in.dtype)
    w_main = (w_main * col_scale).astype(BF16)
    pad = jnp.zeros(w_in.shape[:-1] + (LANES - SM_USED,), w_in.dtype)
    w_small = jnp.concatenate([w_in[..., ik:mq], w_in[..., mi:mo], pad], axis=-1).astype(BF16)
    return w_main, w_small


def _tiles(m, rows_per_batch, per_row):
    if per_row:
        return dict(inproj=m, outproj=m, ffn=m, dsa_q=128, dsa_k=512, mlstm=128, mlstm_batches=1, conv=128)
    t = rows_per_batch
    return dict(inproj=min(1024, t), outproj=min(512, t), ffn=min(1024, t),
                dsa_q=min(128, t), dsa_k=min(512, t), mlstm=min(256, t), mlstm_batches=1, conv=min(512, t))


def _layer(x, ada, p, cache, *, rows_per_batch, alpha):
    m, d = x.shape
    nb = m // rows_per_batch
    t = rows_per_batch
    per_row = cache is not None
    tiles = _tiles(m, t, per_row)
    if per_row:
        mods = [jnp.repeat(v, t, axis=0)[None] for v in jnp.split(ada, 6, axis=-1)]
    else:
        mods = [v[:, None, :] for v in jnp.split(ada, 6, axis=-1)]
    sh1, sc1, g1, sh2, sc2, g2 = mods

    layer = p['layer']
    main, k_rows, v_rows, small = _inproj(x, sh1, sc1, p['w_main'], p['w_small'], layer, per_row, t,
                                          tiles['inproj'])
    k_new = k_rows.reshape(nb, t, A_KV_HEADS, A_HEAD_DIM)
    v_new = v_rows.reshape(nb, t, A_KV_HEADS, A_HEAD_DIM)
    ik_new = small[:, SM_IK:SM_IK + IDX_DIM].reshape(nb, t, IDX_DIM)

    gate_bias = jnp.zeros((1, LANES), F32)
    gate_bias = gate_bias.at[0, SM_MI:SM_MI + M_HEADS].set(p['b_igate'])
    gate_bias = gate_bias.at[0, SM_MF:SM_MF + M_HEADS].set(p['b_fgate'])

    if cache is None:
        main3 = main.reshape(nb, t, MAIN_COLS)
        small3 = small.reshape(nb, t, LANES)
        ki = ik_new.astype(BF16)
        wi = jnp.swapaxes(small3[:, :, SM_IW:SM_IW + IDX_HEADS], 1, 2)
        attn = _dsa(main3, main3, ki, wi, k_col=COL_K // KV_WIDTH, v_col=COL_V // KV_WIDTH,
                    q_off=0, n_keys=t, tq=tiles['dsa_q'], ts=tiles['dsa_k'])
        c0 = jnp.zeros((nb, M_HEADS, M_DIM, M_DIM), F32)
        n0 = jnp.zeros((nb, M_HEADS, 8, M_DIM), F32)
        m0 = jnp.zeros((nb, M_HEADS, 8, LANES), F32)
        mls, c_new, n_new, m_new = _mlstm(main3, small3, gate_bias, p['mlstm_norm_g'], c0, n0, m0,
                                          chunk=tiles['mlstm'], valid_len=tiles['mlstm'],
                                          batches=tiles['mlstm_batches'])
        state = jnp.zeros((nb, CONV_HALO, CONV_CH), F32)
        cnv, conv_new = _conv(main3, state, p['conv_w'], p['conv_b'], p['conv_norm_g'], p['conv_norm_b'],
                              tm=tiles['conv'], valid_len=tiles['conv'])
        attn, mls, cnv = (a.reshape(m, -1) for a in (attn, mls, cnv))
    else:
        ck, cv, cik, c_st, n_st, m_st, conv_st = cache
        past = ck.shape[1]
        n_keys = past + t
        tp = tiles['dsa_q']
        ts = tiles['dsa_k']
        s_pad = -(-n_keys // ts) * ts
        main3 = jnp.pad(main.reshape(nb, t, MAIN_COLS), ((0, 0), (0, tp - t), (0, 0)))
        small3 = jnp.pad(small.reshape(nb, t, LANES), ((0, 0), (0, tp - t), (0, 0)))
        pad_keys = lambda a: jnp.pad(a, ((0, 0), (0, s_pad - n_keys), (0, 0)))
        k_all = pad_keys(jnp.concatenate([ck.reshape(nb, past, KV_WIDTH), k_rows.reshape(nb, t, KV_WIDTH)], axis=1))
        v_all = pad_keys(jnp.concatenate([cv.reshape(nb, past, KV_WIDTH), v_rows.reshape(nb, t, KV_WIDTH)], axis=1))
        kv_all = jnp.concatenate([k_all, v_all], axis=-1).astype(BF16)
        ki = pad_keys(jnp.concatenate([cik, ik_new], axis=1)).astype(BF16)
        wi = jnp.swapaxes(small3[:, :, SM_IW:SM_IW + IDX_HEADS], 1, 2)
        attn = _dsa(main3, kv_all, ki, wi, k_col=0, v_col=1, q_off=past, n_keys=n_keys, tq=tp, ts=ts)
        n0 = jnp.broadcast_to(n_st[:, :, None, :], (nb, M_HEADS, 8, M_DIM))
        m0 = jnp.broadcast_to(m_st[:, :, None, None], (nb, M_HEADS, 8, LANES))
        mls, c_new, n_new, m_new = _mlstm(main3, small3, gate_bias, p['mlstm_norm_g'], c_st, n0, m0,
                                          chunk=tp, valid_len=t, batches=tiles['mlstm_batches'])
        state = jnp.pad(conv_st, ((0, 0), (CONV_PAD, 0), (0, 0)))
        cnv, conv_new = _conv(main3, state, p['conv_w'], p['conv_b'], p['conv_norm_g'], p['conv_norm_b'],
                              tm=tp, valid_len=t)
        attn, mls, cnv = (a[:, :t].reshape(m, -1) for a in (attn, mls, cnv))

    x = _outproj(attn, mls, cnv, x, g1, p['w_out'], layer, p['ln1_g'], p['ln1_b'], per_row, t,
                 tiles['outproj'], alpha)
    x = _ffn(x, sh2, sc2, g2, p['w_ff1'], p['w_ff2'], layer, p['ln2_g'], p['ln2_b'], per_row, t,
             tiles['ffn'], alpha)
    new_state = (k_new, v_new, ik_new, c_new, n_new[:, :, 0, :], m_new[:, :, 0, 0], conv_new[:, CONV_PAD:, :])
    return x, new_state


def kernel(x_prompt, x_sample, c_prompt, c_sample, cache_attn_k, cache_attn_v, cache_idx_k,
           state_mlstm_C, state_mlstm_n, state_mlstm_m, state_conv,
           w_ada, b_ada, w_in, b_igate, b_fgate, mlstm_norm_g, conv_w, conv_b,
           conv_norm_g, conv_norm_b, w_out, ln1_g, ln1_b, w_ff1, w_ff2, ln2_g, ln2_b):
    depth = w_ada.shape[0]
    alpha = (2 * depth) ** 0.25
    bp, tp, d = x_prompt.shape
    bs, tsmp, _ = x_sample.shape

    c_all = jnp.concatenate([c_prompt, c_sample], axis=0)
    rows = -(-c_all.shape[0] // 8) * 8
    ada_all = _ada(jnp.pad(c_all, ((0, rows - c_all.shape[0]), (0, 0))), w_ada, b_ada)

    xp = x_prompt.reshape(bp * tp, d)
    xs = x_sample.reshape(bs * tsmp, d)
    st_p, st_s = [], []
    w_main, w_small = _split_w_in(w_in)
    w_out_b, w_ff1_b, w_ff2_b = w_out.astype(BF16), w_ff1.astype(BF16), w_ff2.astype(BF16)
    for l in range(depth):
        p = {'layer': l, 'w_main': w_main, 'w_small': w_small, 'b_igate': b_igate[l], 'b_fgate': b_fgate[l],
             'mlstm_norm_g': mlstm_norm_g[l], 'conv_w': conv_w[l], 'conv_b': conv_b[l],
             'conv_norm_g': conv_norm_g[l], 'conv_norm_b': conv_norm_b[l],
             'w_out': w_out_b, 'ln1_g': ln1_g[l], 'ln1_b': ln1_b[l],
             'w_ff1': w_ff1_b, 'w_ff2': w_ff2_b,
             'ln2_g': ln2_g[l], 'ln2_b': ln2_b[l]}
        xp, sp = _layer(xp, ada_all[l, :bp], p, None, rows_per_batch=tp, alpha=alpha)
        cache = (cache_attn_k[l], cache_attn_v[l], cache_idx_k[l], state_mlstm_C[l],
                 state_mlstm_n[l], state_mlstm_m[l], state_conv[l])
        xs, ss = _layer(xs, ada_all[l, bp:bp + bs], p, cache, rows_per_batch=tsmp, alpha=alpha)
        st_p.append(sp)
        st_s.append(ss)

    def stack(sts, i):
        return jnp.stack([s[i] for s in sts], axis=0)

    return (xp.reshape(bp, tp, d), xs.reshape(bs, tsmp, d),
            *(stack(st_p, i) for i in range(7)), *(stack(st_s, i) for i in range(7)))
```

```python
import functools

import jax
import jax.numpy as jnp
from jax import lax
from jax.experimental import pallas as pl
from jax.experimental.pallas import tpu as pltpu

F32 = jnp.float32
BF16 = jnp.bfloat16

CHUNK = 64
A_HEADS = 8
A_KV_HEADS = 2
A_GROUP = A_HEADS // A_KV_HEADS
A_HEAD_DIM = 128
A_WIDTH = A_HEADS * A_HEAD_DIM
KV_WIDTH = A_KV_HEADS * A_HEAD_DIM
IDX_HEADS = 8
IDX_DIM = 64
TOPK_MAX = 256
M_HEADS = 4
M_DIM = 128
M_WIDTH = M_HEADS * M_DIM
CONV_CH = 512
CONV_W = 31
LN_EPS = 1e-5
NEG_INF = -1e30
ATTN_LOG2E_SCALE = A_HEAD_DIM ** -0.5 * 1.4426950408889634
INT_MIN = -2 ** 31

LANES = 128
VMEM_LIMIT_BYTES = 56 * 1024 * 1024
ADA_COLS = 1024
FFN_COLS = 512

COL_Q = 0
COL_K = COL_Q + A_WIDTH
COL_V = COL_K + KV_WIDTH
COL_IQ = COL_V + KV_WIDTH
COL_MQ = COL_IQ + IDX_HEADS * IDX_DIM
COL_MK = COL_MQ + M_WIDTH
COL_MV = COL_MK + M_WIDTH
COL_MO = COL_MV + M_WIDTH
COL_CU = COL_MO + M_WIDTH
MAIN_COLS = COL_CU + 2 * CONV_CH
SM_IK = 0
SM_IW = SM_IK + IDX_DIM
SM_MI = SM_IW + IDX_HEADS
SM_MF = SM_MI + M_HEADS
SM_USED = SM_MF + M_HEADS

SLAB = 32 * 8

CONV_HALO = 32
CONV_PAD = CONV_HALO - (CONV_W - 1)
CONV_ROWS = 32


def _cparams(sem):
    return pltpu.CompilerParams(dimension_semantics=sem, vmem_limit_bytes=VMEM_LIMIT_BYTES)


def _normalize(x):
    mu = jnp.mean(x, axis=-1, keepdims=True)
    xc = x - mu
    var = jnp.mean(xc * xc, axis=-1, keepdims=True)
    return xc * lax.rsqrt(var + LN_EPS)


def _sigmoid(x):
    return 1.0 / (1.0 + jnp.exp(-x))


def _dot(a, b):
    return jnp.dot(a, b, preferred_element_type=F32)


def _dot_nt(a, b):
    return lax.dot_general(a, b, (((1,), (1,)), ((), ())), preferred_element_type=F32)


def _ada_kernel(c_ref, w_ref, b_ref, o_ref):
    c = c_ref[...]
    s = (c * _sigmoid(c)).astype(BF16)
    o_ref[0] = _dot(s, w_ref[0].astype(BF16)) + b_ref[0]


def _ada(c_all, w_ada, b_ada):
    depth, d, n = w_ada.shape
    rows = c_all.shape[0]
    tn = ADA_COLS
    return pl.pallas_call(
        _ada_kernel,
        grid=(depth, n // tn),
        in_specs=[pl.BlockSpec((rows, d), lambda l, j: (0, 0)),
                  pl.BlockSpec((1, d, tn), lambda l, j: (l, 0, j)),
                  pl.BlockSpec((1, 1, tn), lambda l, j: (l, 0, j))],
        out_specs=pl.BlockSpec((1, rows, tn), lambda l, j: (l, 0, j)),
        out_shape=jax.ShapeDtypeStruct((depth, rows, n), F32),
        compiler_params=_cparams(("arbitrary", "arbitrary")),
        name="ada",
    )(c_all, w_ada, b_ada.reshape(depth, 1, n))


def _mod_spec(per_row, tm, rows_per_batch, d):
    if per_row:
        return pl.BlockSpec((1, tm, d), lambda i, *_: (0, i, 0))
    bpb = rows_per_batch // tm
    return pl.BlockSpec((1, 1, d), lambda i, *_: (i // bpb, 0, 0))


def _inproj_kernel(x_ref, sh_ref, sc_ref, wm_ref, ws_ref, main_ref, k_ref, v_ref, small_ref, h_scr, *, kv_block):
    j = pl.program_id(1)

    @pl.when(j == 0)
    def _():
        h = _normalize(x_ref[...]) * (1.0 + sc_ref[0]) + sh_ref[0]
        hb = h.astype(BF16)
        h_scr[...] = hb
        small_ref[...] = _dot(hb, ws_ref[...])

    acc = _dot(h_scr[...], wm_ref[...])
    main_ref[...] = acc.astype(BF16)

    @pl.when(j == kv_block)
    def _():
        k_ref[...] = acc[:, :KV_WIDTH]
        v_ref[...] = acc[:, KV_WIDTH:]


def _inproj(x, sh, sc, w_main, w_small, layer, per_row, rows_per_batch, tm):
    m, d = x.shape
    tn = 2 * KV_WIDTH
    mod = _mod_spec(per_row, tm, rows_per_batch, d)
    return pl.pallas_call(
        functools.partial(_inproj_kernel, kv_block=COL_K // tn),
        grid=(m // tm, MAIN_COLS // tn),
        in_specs=[pl.BlockSpec((tm, d), lambda i, j: (i, 0)), mod, mod,
                  pl.BlockSpec((None, d, tn), lambda i, j: (layer, 0, j)),
                  pl.BlockSpec((None, d, LANES), lambda i, j: (layer, 0, 0))],
        out_specs=[pl.BlockSpec((tm, tn), lambda i, j: (i, j)),
                   pl.BlockSpec((tm, KV_WIDTH), lambda i, j: (i, 0)),
                   pl.BlockSpec((tm, KV_WIDTH), lambda i, j: (i, 0)),
                   pl.BlockSpec((tm, LANES), lambda i, j: (i, 0))],
        out_shape=[jax.ShapeDtypeStruct((m, MAIN_COLS), BF16),
                   jax.ShapeDtypeStruct((m, KV_WIDTH), F32),
                   jax.ShapeDtypeStruct((m, KV_WIDTH), F32),
                   jax.ShapeDtypeStruct((m, LANES), F32)],
        scratch_shapes=[pltpu.VMEM((tm, d), BF16)],
        compiler_params=_cparams(("arbitrary", "arbitrary")),
        name="inproj",
    )(x, sh, sc, w_main, w_small)


def _outproj_kernel(a_ref, m_ref, c_ref, x_ref, g_ref, w_ref, lg_ref, lb_ref, o_ref, *, alpha):
    y = _dot(a_ref[...], w_ref[0:A_WIDTH, :])
    y += _dot(m_ref[...], w_ref[A_WIDTH:A_WIDTH + M_WIDTH, :])
    y += _dot(c_ref[...], w_ref[A_WIDTH + M_WIDTH:, :])
    z = alpha * x_ref[...] + (1.0 + g_ref[0]) * y
    o_ref[...] = _normalize(z) * lg_ref[...] + lb_ref[...]


def _outproj(attn, mls, cnv, x, gate, w_out, layer, ln_g, ln_b, per_row, rows_per_batch, tm, alpha):
    m, d = x.shape
    row = lambda w: pl.BlockSpec((tm, w), lambda i: (i, 0))
    vec = pl.BlockSpec((1, d), lambda i: (0, 0))
    return pl.pallas_call(
        functools.partial(_outproj_kernel, alpha=alpha),
        grid=(m // tm,),
        in_specs=[row(A_WIDTH), row(M_WIDTH), row(CONV_CH), row(d),
                  _mod_spec(per_row, tm, rows_per_batch, d),
                  pl.BlockSpec((None,) + w_out.shape[1:], lambda i: (layer, 0, 0)), vec, vec],
        out_specs=row(d),
        out_shape=jax.ShapeDtypeStruct((m, d), F32),
        compiler_params=_cparams(("arbitrary",)),
        name="outproj",
    )(attn, mls, cnv, x, gate, w_out, ln_g.reshape(1, d), ln_b.reshape(1, d))


def _ffn_kernel(x_ref, sh_ref, sc_ref, g_ref, w1_ref, w2_ref, lg_ref, lb_ref, o_ref, h_scr, *, alpha):
    j = pl.program_id(1)

    @pl.when(j == 0)
    def _():
        h = _normalize(x_ref[...]) * (1.0 + sc_ref[0]) + sh_ref[0]
        h_scr[...] = h.astype(BF16)
        o_ref[...] = jnp.zeros_like(o_ref)

    u = jnp.maximum(_dot(h_scr[...], w1_ref[...]), 0.0)
    o_ref[...] += _dot((u * u).astype(BF16), w2_ref[...])

    @pl.when(j == pl.num_programs(1) - 1)
    def _():
        z = alpha * x_ref[...] + (1.0 + g_ref[0]) * o_ref[...]
        o_ref[...] = _normalize(z) * lg_ref[...] + lb_ref[...]


def _ffn(x, sh, sc, gate, w1, w2, layer, ln_g, ln_b, per_row, rows_per_batch, tm, alpha):
    m, d = x.shape
    dff = w1.shape[2]
    tf = FFN_COLS
    mod = _mod_spec(per_row, tm, rows_per_batch, d)
    vec = pl.BlockSpec((1, d), lambda i, j: (0, 0))
    return pl.pallas_call(
        functools.partial(_ffn_kernel, alpha=alpha),
        grid=(m // tm, dff // tf),
        in_specs=[pl.BlockSpec((tm, d), lambda i, j: (i, 0)), mod, mod, mod,
                  pl.BlockSpec((None, d, tf), lambda i, j: (layer, 0, j)),
                  pl.BlockSpec((None, tf, d), lambda i, j: (layer, j, 0)), vec, vec],
        out_specs=pl.BlockSpec((tm, d), lambda i, j: (i, 0)),
        out_shape=jax.ShapeDtypeStruct((m, d), F32),
        scratch_shapes=[pltpu.VMEM((tm, d), BF16)],
        compiler_params=_cparams(("arbitrary", "arbitrary")),
        name="ffn",
    )(x, sh, sc, gate, w1, w2, ln_g.reshape(1, d), ln_b.reshape(1, d))


def _dsa_kernel(q_ref, qi_ref, wi_ref, k_ref, v_ref, ki_ref, o_ref,
                key_scr, plane_scr, alive_scr, qs_scr, qis_scr, s_scr, m_scr, l_scr, acc_scr,
                *, tq, ts, q_off, n_keys, topk, n_tiles_static):
    qb = pl.program_id(1)
    q_pos0 = q_off + qb * tq
    if n_tiles_static is None:
        vis_max = jnp.minimum(((q_pos0 + tq - 1) // CHUNK + 1) * CHUNK, n_keys)
        n_tiles = (vis_max + ts - 1) // ts
    else:
        n_tiles = n_tiles_static
    lane_tiles = [slice(c * LANES, (c + 1) * LANES) for c in range(ts // LANES)]
    n_slabs = ts // SLAB

    q_chunk = (q_pos0 + lax.broadcasted_iota(jnp.int32, (1, tq), 1)) // CHUNK
    n_vis = jnp.minimum((q_chunk + 1) * CHUNK, n_keys)
    kk = jnp.minimum(topk, n_vis).astype(F32)

    eye = (lax.broadcasted_iota(jnp.int32, (tq, tq), 0)
           == lax.broadcasted_iota(jnp.int32, (tq, tq), 1)).astype(BF16)
    for h in range(A_HEADS):
        rows = slice((h % A_GROUP) * tq, (h % A_GROUP + 1) * tq)
        qs_scr[h // A_GROUP, rows, 0:A_HEAD_DIM] = q_ref[0, :, h * A_HEAD_DIM:(h + 1) * A_HEAD_DIM]
        qs_scr[h // A_GROUP, rows, A_HEAD_DIM:A_HEAD_DIM + tq] = eye
    for h in range(IDX_HEADS):
        qis_scr[h // 2, (h % 2) * tq:(h % 2 + 1) * tq, :] = qi_ref[0, :, h * IDX_DIM:(h + 1) * IDX_DIM]
    wi = wi_ref[0] * IDX_HEADS ** -0.5 * IDX_DIM ** -0.5

    def score_tile(t):
        start = pl.multiple_of(t * ts, ts)
        kt = ki_ref[0, pl.ds(start, ts), :]
        score = None
        for pair in range(IDX_HEADS // 2):
            d = _dot_nt(kt, qis_scr[pair])
            for e in range(2):
                h = 2 * pair + e
                term = wi[h:h + 1, :] * jnp.maximum(d[:, e * tq:(e + 1) * tq], 0.0)
                score = term if score is None else score + term
        score = jnp.where(score == 0.0, 0.0, score)
        bits = lax.bitcast_convert_type(score, jnp.int32)
        key = bits ^ ((bits >> 31) & 0x7FFFFFFF)
        visible = lax.broadcasted_iota(jnp.int32, (ts, tq), 0) < n_vis - start
        key = jnp.where(visible, key, INT_MIN)
        key_scr[t] = key
        for sl in range(n_slabs):
            a = [key[sl * SLAB + 8 * j:sl * SLAB + 8 * j + 8, :] ^ INT_MIN for j in range(32)]
            dist, mask = 16, 0x0000FFFF
            while dist:
                k = 0
                while k < 32:
                    swap = (a[k] ^ lax.shift_right_logical(a[k + dist], dist)) & mask
                    a[k] = a[k] ^ swap
                    a[k + dist] = a[k + dist] ^ (swap << dist)
                    k = (k + dist + 1) & ~dist
                dist >>= 1
                mask ^= (mask << dist) & 0xFFFFFFFF
            present = a[0]
            for i in range(32):
                plane_scr[t, sl, i] = a[i]
                present = present | a[i]
            alive_scr[t, sl] = present

    def score_pair(i, carry):
        score_tile(2 * i)
        score_tile(2 * i + 1)
        return carry

    lax.fori_loop(0, n_tiles // 2, score_pair, 0)

    @pl.when(n_tiles % 2 == 1)
    def _():
        score_tile(n_tiles - 1)

    @pl.when(n_tiles % 2 == 1)
    def _():
        plane_scr[n_tiles] = jnp.zeros(plane_scr.shape[1:], jnp.int32)
        alive_scr[n_tiles] = jnp.zeros(alive_scr.shape[1:], jnp.int32)

    def sweep(i, take_prev):
        def body(pair, accs):
            accs = list(accs)
            for e in range(2):
                t = 2 * pair + e
                for sl in range(n_slabs):
                    a = alive_scr[t, sl]
                    if take_prev is not None:
                        x = a & plane_scr[t, sl, i - 1]
                        a = jnp.where(take_prev, x, a ^ x)
                        alive_scr[t, sl] = a
                    accs[e] = accs[e] + lax.population_count(a & plane_scr[t, sl, i])
            return tuple(accs)
        zero = jnp.zeros((8, tq), jnp.int32)
        acc0, acc1 = lax.fori_loop(0, (n_tiles + 1) // 2, body, (zero, zero))
        return jnp.sum((acc0 + acc1).astype(F32), axis=0, keepdims=True)

    def decide(i, cnt, want, n_alive, t_u):
        take = cnt >= want
        want = jnp.where(take, want, want - cnt)
        n_alive = jnp.where(take, cnt, n_alive - cnt)
        t_u = t_u | jnp.where(take, lax.shift_left(jnp.int32(1), 31 - i), 0)
        return take.astype(jnp.int32), want, n_alive, t_u

    def radix_step(i, carry):
        take_prev, want, n_alive, t_u = carry
        cnt = sweep(i, jnp.broadcast_to(take_prev, (8, tq)) != 0)
        return decide(i, cnt, want, n_alive, t_u)

    carry = decide(0, sweep(0, None), kk, n_vis.astype(F32), jnp.zeros((1, tq), jnp.int32))
    _, keep, n_alive, t_u = lax.fori_loop(1, 32, radix_step, carry)
    thr = jnp.broadcast_to(t_u ^ INT_MIN, (8, tq))

    @pl.when(jnp.max(n_alive - keep) > 0.0)
    def _():
        lower = (lax.broadcasted_iota(jnp.int32, (ts, ts), 0)
                 >= lax.broadcasted_iota(jnp.int32, (ts, ts), 1)).astype(BF16)

        def tie_tile(t, seen):
            blk = key_scr[t]
            eq = blk == thr[0:1, :]
            rank = seen + _dot(lower, jnp.where(eq, 1.0, 0.0).astype(BF16))
            key_scr[t] = jnp.where(eq & (rank > keep), INT_MIN, blk)
            return rank[ts - 1:ts, :]

        lax.fori_loop(0, n_tiles, tie_tile, jnp.zeros((1, tq), F32))

    m_scr[...] = jnp.full(m_scr.shape, NEG_INF, F32)
    l_scr[...] = jnp.zeros(l_scr.shape, F32)
    acc_scr[...] = jnp.zeros(acc_scr.shape, F32)

    def logits_tile(t, slot):
        start = pl.multiple_of(t * ts, ts)
        bias = jnp.where(key_scr[t] >= thr[0:1, :], 0.0, NEG_INF).astype(BF16)
        for n in range(A_KV_HEADS):
            kt = k_ref[0, pl.ds(start, ts), n * A_HEAD_DIM:(n + 1) * A_HEAD_DIM]
            s_scr[slot, n] = _dot_nt(qs_scr[n], jnp.concatenate([kt, bias], axis=1))

    def softmax_pv_tile(t, slot):
        start = pl.multiple_of(t * ts, ts)
        for n in range(A_KV_HEADS):
            vt = v_ref[0, pl.ds(start, ts), n * A_HEAD_DIM:(n + 1) * A_HEAD_DIM]
            for g in range(A_GROUP):
                h = n * A_GROUP + g
                s = [s_scr[slot, n, g * tq:(g + 1) * tq, ls] for ls in lane_tiles]
                mx = s[0]
                for sc in s[1:]:
                    mx = jnp.maximum(mx, sc)
                m_old = m_scr[h]
                m_new = jnp.maximum(m_old, jnp.max(mx, axis=1, keepdims=True))
                alpha = jnp.exp2(m_old - m_new)
                p = [jnp.exp2(sc - m_new) for sc in s]
                psum = p[0]
                for pc in p[1:]:
                    psum = psum + pc
                l_scr[h] = alpha * l_scr[h] + psum
                pv = _dot(jnp.concatenate(p, axis=1).astype(BF16), vt)
                acc_scr[h] = alpha * acc_scr[h] + pv
                m_scr[h] = m_new

    def attend_pair(i, carry):
        t = 2 * i
        logits_tile(t + 1, 1)
        softmax_pv_tile(t, 0)
        logits_tile(t + 2, 0)
        softmax_pv_tile(t + 1, 1)
        return carry

    n_pairs = (n_tiles - 1) // 2
    logits_tile(0, 0)
    lax.fori_loop(0, n_pairs, attend_pair, 0)
    t_rest = 2 * n_pairs

    @pl.when(t_rest + 1 < n_tiles)
    def _():
        logits_tile(t_rest + 1, 1)
        softmax_pv_tile(t_rest, 0)
        softmax_pv_tile(t_rest + 1, 1)

    @pl.when(t_rest + 1 >= n_tiles)
    def _():
        softmax_pv_tile(t_rest, 0)

    for h in range(A_HEADS):
        l_tot = jnp.sum(l_scr[h], axis=1, keepdims=True)
        o_ref[0, :, h * A_HEAD_DIM:(h + 1) * A_HEAD_DIM] = (acc_scr[h] / l_tot).astype(BF16)


def _dsa(main3, keys_src, ki, wi, *, k_col, v_col, q_off, n_keys, tq, ts):
    assert tq == A_HEAD_DIM, "the mask rides in the unused half of a 2*A_HEAD_DIM-deep contraction"
    b, t_q, _ = main3.shape
    s_len = keys_src.shape[1]
    topk = min(TOPK_MAX, n_keys // 4)
    dynamic = q_off == 0
    kern = functools.partial(_dsa_kernel, tq=tq, ts=ts, q_off=q_off, n_keys=n_keys, topk=topk,
                             n_tiles_static=None if dynamic else s_len // ts)
    return pl.pallas_call(
        kern,
        grid=(b, t_q // tq),
        in_specs=[pl.BlockSpec((1, tq, A_WIDTH), lambda i, j: (i, j, COL_Q // A_WIDTH)),
                  pl.BlockSpec((1, tq, IDX_HEADS * IDX_DIM), lambda i, j: (i, j, COL_IQ // (IDX_HEADS * IDX_DIM))),
                  pl.BlockSpec((1, IDX_HEADS, tq), lambda i, j: (i, 0, j)),
                  pl.BlockSpec((1, s_len, KV_WIDTH), lambda i, j: (i, 0, k_col)),
                  pl.BlockSpec((1, s_len, KV_WIDTH), lambda i, j: (i, 0, v_col)),
                  pl.BlockSpec((1, s_len, IDX_DIM), lambda i, j: (i, 0, 0))],
        out_specs=pl.BlockSpec((1, tq, A_WIDTH), lambda i, j: (i, j, 0)),
        out_shape=jax.ShapeDtypeStruct((b, t_q, A_WIDTH), BF16),
        scratch_shapes=[pltpu.VMEM((s_len // ts, ts, tq), jnp.int32),
                        pltpu.VMEM((s_len // ts + 1, ts // SLAB, 32, 8, tq), jnp.int32),
                        pltpu.VMEM((s_len // ts + 1, ts // SLAB, 8, tq), jnp.int32),
                        pltpu.VMEM((A_KV_HEADS, A_GROUP * tq, A_HEAD_DIM + tq), BF16),
                        pltpu.VMEM((IDX_HEADS // 2, 2 * tq, IDX_DIM), BF16),
                        pltpu.VMEM((2, A_KV_HEADS, A_GROUP * tq, ts), F32),
                        pltpu.VMEM((A_HEADS, tq, LANES), F32),
                        pltpu.VMEM((A_HEADS, tq, LANES), F32),
                        pltpu.VMEM((A_HEADS, tq, A_HEAD_DIM), F32)],
        compiler_params=_cparams(("arbitrary", "arbitrary")),
        name="dsa",
    )(main3, main3, wi, keys_src, keys_src, ki)


def _log_sigmoid(x):
    return jnp.minimum(x, 0.0) - jnp.log1p(jnp.exp(-jnp.abs(x)))


def _mlstm_kernel(q_ref, k_ref, v_ref, o_ref, g_ref, gb_ref, ng_ref, c0_ref, n0_ref, m0_ref,
                  y_ref, c_out, n_out, m_out, c_scr, n_scr, m_scr, *, chunk, valid_len, batches):
    step = pl.program_id(1)

    @pl.when(step == 0)
    def _():
        c_scr[...] = c0_ref[...]
        n_scr[...] = n0_ref[...]
        m_scr[...] = m0_ref[...]

    for bi in range(batches):
        _mlstm_chunk(bi, q_ref, k_ref, v_ref, o_ref, g_ref, gb_ref, ng_ref, y_ref, c_scr, n_scr, m_scr,
                     chunk=chunk, valid_len=valid_len)

    c_out[...] = c_scr[...]
    n_out[...] = n_scr[...]
    m_out[...] = m_scr[...]


def _mlstm_chunk(bi, q_ref, k_ref, v_ref, o_ref, g_ref, gb_ref, ng_ref, y_ref, c_scr, n_scr, m_scr,
                 *, chunk, valid_len):
    L = chunk
    g = g_ref[bi] + gb_ref[...]
    lf = _log_sigmoid(g)
    if valid_len < L:
        live = lax.broadcasted_iota(jnp.int32, (L, 1), 0) < valid_len
        lf = jnp.where(live, lf, 0.0)
        g = jnp.where(live, g, NEG_INF)
    g_t = g.T
    lf_t = lf.T
    row = lax.broadcasted_iota(jnp.int32, (L, L), 0)
    col = lax.broadcasted_iota(jnp.int32, (L, L), 1)
    lower = row >= col
    b_c = jnp.dot(jnp.where(lower, 1.0, 0.0), lf, precision=lax.Precision.HIGHEST,
                  preferred_element_type=F32)
    b_r = jnp.dot(lf_t, jnp.where(lower, 0.0, 1.0) + jnp.where(row == col, 1.0, 0.0),
                  precision=lax.Precision.HIGHEST, preferred_element_type=F32)

    for h in range(M_HEADS):
        sl = slice(h * M_DIM, (h + 1) * M_DIM)
        b_col = b_c[:, SM_MF + h:SM_MF + h + 1]
        ig_col = g[:, SM_MI + h:SM_MI + h + 1]
        a_row = g_t[SM_MI + h:SM_MI + h + 1, :] - b_r[SM_MF + h:SM_MF + h + 1, :]
        m_prev = m_scr[bi, h, 0:1, 0:1]
        d_mat = jnp.where(lower, b_col + a_row, NEG_INF)
        inter = b_col + m_prev
        m_t = jnp.maximum(inter, jnp.max(d_mat, axis=1, keepdims=True))
        w_intra = jnp.exp(d_mat - m_t)
        w_inter = jnp.exp(inter - m_t)
        qh = q_ref[bi, :, sl]
        kf = k_ref[bi, :, sl].astype(F32) * M_DIM ** -0.5
        vh = v_ref[bi, :, sl]
        s = _dot_nt(qh, kf.astype(BF16)) * w_intra
        c_old = c_scr[bi, h]
        n_old = n_scr[bi, h, 0:1, :]
        num = w_inter * _dot(qh, c_old.astype(BF16)) + _dot(s.astype(BF16), vh)
        den = (w_inter * jnp.sum(qh.astype(F32) * n_old, axis=1, keepdims=True)
               + jnp.sum(s, axis=1, keepdims=True))
        hid = num / jnp.maximum(jnp.abs(den), jnp.exp(-m_t))
        m_end = m_t[L - 1:L, :]
        b_last = b_col[L - 1:L, :]
        w_end = jnp.exp(b_last - b_col + ig_col - m_end)
        decay = jnp.exp(b_last + m_prev - m_end)
        kw = w_end * kf
        c_scr[bi, h] = decay * c_old + _dot(kw.T.astype(BF16), vh)
        n_scr[bi, h] = jnp.broadcast_to(decay * n_old + jnp.sum(kw, axis=0, keepdims=True), (8, M_DIM))
        m_scr[bi, h] = jnp.broadcast_to(m_end, (8, LANES))
        gated = _sigmoid(o_ref[bi, :, sl].astype(F32)) * hid
        y_ref[bi, :, sl] = (_normalize(gated) * ng_ref[:, sl]).astype(BF16)


def _mlstm(main3, small3, gate_bias, norm_g, c0, n0, m0, *, chunk, valid_len, batches):
    b, t, _ = main3.shape
    bb = batches
    blk = lambda col: pl.BlockSpec((bb, chunk, M_WIDTH), lambda i, j: (i, j, col // M_WIDTH))
    st = lambda shape: pl.BlockSpec((bb,) + shape, lambda i, j: (i,) + (0,) * len(shape))
    c_shape, n_shape, m_shape = (M_HEADS, M_DIM, M_DIM), (M_HEADS, 8, M_DIM), (M_HEADS, 8, LANES)
    return pl.pallas_call(
        functools.partial(_mlstm_kernel, chunk=chunk, valid_len=valid_len, batches=bb),
        grid=(b // bb, t // chunk),
        in_specs=[blk(COL_MQ), blk(COL_MK), blk(COL_MV), blk(COL_MO),
                  pl.BlockSpec((bb, chunk, LANES), lambda i, j: (i, j, 0)),
                  pl.BlockSpec((1, LANES), lambda i, j: (0, 0)),
                  pl.BlockSpec((1, M_WIDTH), lambda i, j: (0, 0)),
                  st(c_shape), st(n_shape), st(m_shape)],
        out_specs=[pl.BlockSpec((bb, chunk, M_WIDTH), lambda i, j: (i, j, 0)),
                   st(c_shape), st(n_shape), st(m_shape)],
        out_shape=[jax.ShapeDtypeStruct((b, t, M_WIDTH), BF16),
                   jax.ShapeDtypeStruct((b,) + c_shape, F32),
                   jax.ShapeDtypeStruct((b,) + n_shape, F32),
                   jax.ShapeDtypeStruct((b,) + m_shape, F32)],
        scratch_shapes=[pltpu.VMEM((bb,) + c_shape, F32), pltpu.VMEM((bb,) + n_shape, F32),
                        pltpu.VMEM((bb,) + m_shape, F32)],
        compiler_params=_cparams(("arbitrary", "arbitrary")),
        name="mlstm",
    )(main3, main3, main3, main3, small3, gate_bias, norm_g.reshape(1, M_WIDTH), c0, n0, m0)


def _conv_kernel(cu_ref, halo_ref, st_ref, w_ref, cb_ref, ng_ref, nb_ref, y_ref, ns_ref, full_scr, sh_scr,
                 *, tm, valid_len):
    i = pl.program_id(1)

    def glu(u):
        return u[:, :CONV_CH].astype(F32) * _sigmoid(u[:, CONV_CH:].astype(F32))

    @pl.when(i == 0)
    def _():
        full_scr[0:CONV_HALO, :] = st_ref[0]

    @pl.when(i > 0)
    def _():
        full_scr[0:CONV_HALO, :] = glu(halo_ref[0])

    full_scr[CONV_HALO:CONV_HALO + tm, :] = glu(cu_ref[0])
    span = tm + CONV_HALO - 8
    for b in range(1, 8):
        sh_scr[b - 1] = full_scr[pl.ds(b, span), :]
    for c0 in range(0, tm, CONV_ROWS):
        acc = None
        for j in range(CONV_W):
            a8, b = (CONV_PAD + j) // 8 * 8, (CONV_PAD + j) % 8
            if b == 0:
                src = full_scr[pl.ds(c0 + a8, CONV_ROWS), :]
            else:
                src = sh_scr[b - 1, pl.ds(c0 + a8, CONV_ROWS), :]
            term = w_ref[j:j + 1, :] * src
            acc = term if acc is None else acc + term
        y = _normalize(acc + cb_ref[...]) * ng_ref[...] + nb_ref[...]
        y_ref[0, c0:c0 + CONV_ROWS, :] = (y * _sigmoid(y)).astype(BF16)
    ns_ref[0] = full_scr[pl.ds(valid_len, CONV_HALO), :]


def _conv(main3, state, w, bias, ng, nb, *, tm, valid_len):
    b, t, _ = main3.shape
    cu_blk = COL_CU // (2 * CONV_CH)
    hpb = tm // CONV_HALO
    vec = pl.BlockSpec((1, CONV_CH), lambda i, j: (0, 0))
    return pl.pallas_call(
        functools.partial(_conv_kernel, tm=tm, valid_len=valid_len),
        grid=(b, t // tm),
        in_specs=[pl.BlockSpec((1, tm, 2 * CONV_CH), lambda i, j: (i, j, cu_blk)),
                  pl.BlockSpec((1, CONV_HALO, 2 * CONV_CH), lambda i, j: (i, jnp.maximum(j * hpb - 1, 0), cu_blk)),
                  pl.BlockSpec((1, CONV_HALO, CONV_CH), lambda i, j: (i, 0, 0)),
                  pl.BlockSpec((CONV_W, CONV_CH), lambda i, j: (0, 0)), vec, vec, vec],
        out_specs=[pl.BlockSpec((1, tm, CONV_CH), lambda i, j: (i, j, 0)),
                   pl.BlockSpec((1, CONV_HALO, CONV_CH), lambda i, j: (i, 0, 0))],
        out_shape=[jax.ShapeDtypeStruct((b, t, CONV_CH), BF16),
                   jax.ShapeDtypeStruct((b, CONV_HALO, CONV_CH), F32)],
        scratch_shapes=[pltpu.VMEM((CONV_HALO + tm, CONV_CH), F32),
                        pltpu.VMEM((7, CONV_HALO + tm - 8, CONV_CH), F32)],
        compiler_params=_cparams(("arbitrary", "arbitrary")),
        name="conv",
    )(main3, main3, state, w, bias.reshape(1, CONV_CH), ng.reshape(1, CONV_CH), nb.reshape(1, CONV_CH))


def _split_w_in(w_in):
    sizes = (A_WIDTH, KV_WIDTH, KV_WIDTH, IDX_HEADS * IDX_DIM, IDX_DIM, IDX_HEADS,
             M_WIDTH, M_WIDTH, M_WIDTH, M_HEADS, M_HEADS, M_WIDTH, 2 * CONV_CH)
    offs = [0]
    for size in sizes:
        offs.append(offs[-1] + size)
    aq, ak, av, iq, ik, iw, mq, mk, mv, mi, mf, mo, cu, end = offs
    w_main = jnp.concatenate([w_in[..., aq:ik], w_in[..., mq:mi], w_in[..., mo:end]], axis=-1)
    col_scale = jnp.where(jnp.arange(MAIN_COLS) < A_WIDTH, ATTN_LOG2E_SCALE, 1.0).astype(w_in.dtype)
    w_main = (w_main * col_scale).astype(BF16)
    pad = jnp.zeros(w_in.shape[:-1] + (LANES - SM_USED,), w_in.dtype)
    w_small = jnp.concatenate([w_in[..., ik:mq], w_in[..., mi:mo], pad], axis=-1).astype(BF16)
    return w_main, w_small


def _tiles(m, rows_per_batch, per_row):
    if per_row:
        return dict(inproj=m, outproj=m, ffn=m, dsa_q=128, dsa_k=512, mlstm=128, mlstm_batches=1, conv=128)
    t = rows_per_batch
    return dict(inproj=min(1024, t), outproj=min(512, t), ffn=min(1024, t),
                dsa_q=min(128, t), dsa_k=min(512, t), mlstm=min(256, t), mlstm_batches=1, conv=min(512, t))


def _layer(x, ada, p, cache, *, rows_per_batch, alpha):
    m, d = x.shape
    nb = m // rows_per_batch
    t = rows_per_batch
    per_row = cache is not None
    tiles = _tiles(m, t, per_row)
    if per_row:
        mods = [jnp.repeat(v, t, axis=0)[None] for v in jnp.split(ada, 6, axis=-1)]
    else:
        mods = [v[:, None, :] for v in jnp.split(ada, 6, axis=-1)]
    sh1, sc1, g1, sh2, sc2, g2 = mods

    layer = p['layer']
    main, k_rows, v_rows, small = _inproj(x, sh1, sc1, p['w_main'], p['w_small'], layer, per_row, t,
                                          tiles['inproj'])
    k_new = k_rows.reshape(nb, t, A_KV_HEADS, A_HEAD_DIM)
    v_new = v_rows.reshape(nb, t, A_KV_HEADS, A_HEAD_DIM)
    ik_new = small[:, SM_IK:SM_IK + IDX_DIM].reshape(nb, t, IDX_DIM)

    gate_bias = jnp.zeros((1, LANES), F32)
    gate_bias = gate_bias.at[0, SM_MI:SM_MI + M_HEADS].set(p['b_igate'])
    gate_bias = gate_bias.at[0, SM_MF:SM_MF + M_HEADS].set(p['b_fgate'])

    if cache is None:
        main3 = main.reshape(nb, t, MAIN_COLS)
        small3 = small.reshape(nb, t, LANES)
        ki = ik_new.astype(BF16)
        wi = jnp.swapaxes(small3[:, :, SM_IW:SM_IW + IDX_HEADS], 1, 2)
        attn = _dsa(main3, main3, ki, wi, k_col=COL_K // KV_WIDTH, v_col=COL_V // KV_WIDTH,
                    q_off=0, n_keys=t, tq=tiles['dsa_q'], ts=tiles['dsa_k'])
        c0 = jnp.zeros((nb, M_HEADS, M_DIM, M_DIM), F32)
        n0 = jnp.zeros((nb, M_HEADS, 8, M_DIM), F32)
        m0 = jnp.zeros((nb, M_HEADS, 8, LANES), F32)
        mls, c_new, n_new, m_new = _mlstm(main3, small3, gate_bias, p['mlstm_norm_g'], c0, n0, m0,
                                          chunk=tiles['mlstm'], valid_len=tiles['mlstm'],
                                          batches=tiles['mlstm_batches'])
        state = jnp.zeros((nb, CONV_HALO, CONV_CH), F32)
        cnv, conv_new = _conv(main3, state, p['conv_w'], p['conv_b'], p['conv_norm_g'], p['conv_norm_b'],
                              tm=tiles['conv'], valid_len=tiles['conv'])
        attn, mls, cnv = (a.reshape(m, -1) for a in (attn, mls, cnv))
    else:
        ck, cv, cik, c_st, n_st, m_st, conv_st = cache
        past = ck.shape[1]
        n_keys = past + t
        tp = tiles['dsa_q']
        ts = tiles['dsa_k']
        s_pad = -(-n_keys // ts) * ts
        main3 = jnp.pad(main.reshape(nb, t, MAIN_COLS), ((0, 0), (0, tp - t), (0, 0)))
        small3 = jnp.pad(small.reshape(nb, t, LANES), ((0, 0), (0, tp - t), (0, 0)))
        pad_keys = lambda a: jnp.pad(a, ((0, 0), (0, s_pad - n_keys), (0, 0)))
        k_all = pad_keys(jnp.concatenate([ck.reshape(nb, past, KV_WIDTH), k_rows.reshape(nb, t, KV_WIDTH)], axis=1))
        v_all = pad_keys(jnp.concatenate([cv.reshape(nb, past, KV_WIDTH), v_rows.reshape(nb, t, KV_WIDTH)], axis=1))
        kv_all = jnp.concatenate([k_all, v_all], axis=-1).astype(BF16)
        ki = pad_keys(jnp.concatenate([cik, ik_new], axis=1)).astype(BF16)
        wi = jnp.swapaxes(small3[:, :, SM_IW:SM_IW + IDX_HEADS], 1, 2)
        attn = _dsa(main3, kv_all, ki, wi, k_col=0, v_col=1, q_off=past, n_keys=n_keys, tq=tp, ts=ts)
        n0 = jnp.broadcast_to(n_st[:, :, None, :], (nb, M_HEADS, 8, M_DIM))
        m0 = jnp.broadcast_to(m_st[:, :, None, None], (nb, M_HEADS, 8, LANES))
        mls, c_new, n_new, m_new = _mlstm(main3, small3, gate_bias, p['mlstm_norm_g'], c_st, n0, m0,
                                          chunk=tp, valid_len=t, batches=tiles['mlstm_batches'])
        state = jnp.pad(conv_st, ((0, 0), (CONV_PAD, 0), (0, 0)))
        cnv, conv_new = _conv(main3, state, p['conv_w'], p['conv_b'], p['conv_norm_g'], p['conv_norm_b'],
                              tm=tp, valid_len=t)
        attn, mls, cnv = (a[:, :t].reshape(m, -1) for a in (attn, mls, cnv))

    x = _outproj(attn, mls, cnv, x, g1, p['w_out'], layer, p['ln1_g'], p['ln1_b'], per_row, t,
                 tiles['outproj'], alpha)
    x = _ffn(x, sh2, sc2, g2, p['w_ff1'], p['w_ff2'], layer, p['ln2_g'], p['ln2_b'], per_row, t,
             tiles['ffn'], alpha)
    new_state = (k_new, v_new, ik_new, c_new, n_new[:, :, 0, :], m_new[:, :, 0, 0], conv_new[:, CONV_PAD:, :])
    return x, new_state


def kernel(x_prompt, x_sample, c_prompt, c_sample, cache_attn_k, cache_attn_v, cache_idx_k,
           state_mlstm_C, state_mlstm_n, state_mlstm_m, state_conv,
           w_ada, b_ada, w_in, b_igate, b_fgate, mlstm_norm_g, conv_w, conv_b,
           conv_norm_g, conv_norm_b, w_out, ln1_g, ln1_b, w_ff1, w_ff2, ln2_g, ln2_b):
    depth = w_ada.shape[0]
    alpha = (2 * depth) ** 0.25
    bp, tp, d = x_prompt.shape
    bs, tsmp, _ = x_sample.shape

    c_all = jnp.concatenate([c_prompt, c_sample], axis=0)
    rows = -(-c_all.shape[0] // 8) * 8
    ada_all = _ada(jnp.pad(c_all, ((0, rows - c_all.shape[0]), (0, 0))), w_ada, b_ada)

    xp = x_prompt.reshape(bp * tp, d)
    xs = x_sample.reshape(bs * tsmp, d)
    st_p, st_s = [], []
    w_main, w_small = _split_w_in(w_in)
    w_out_b, w_ff1_b, w_ff2_b = w_out.astype(BF16), w_ff1.astype(BF16), w_ff2.astype(BF16)
    for l in range(depth):
        p = {'layer': l, 'w_main': w_main, 'w_small': w_small, 'b_igate': b_igate[l], 'b_fgate': b_fgate[l],
             'mlstm_norm_g': mlstm_norm_g[l], 'conv_w': conv_w[l], 'conv_b': conv_b[l],
             'conv_norm_g': conv_norm_g[l], 'conv_norm_b': conv_norm_b[l],
             'w_out': w_out_b, 'ln1_g': ln1_g[l], 'ln1_b': ln1_b[l],
             'w_ff1': w_ff1_b, 'w_ff2': w_ff2_b,
             'ln2_g': ln2_g[l], 'ln2_b': ln2_b[l]}
        xp, sp = _layer(xp, ada_all[l, :bp], p, None, rows_per_batch=tp, alpha=alpha)
        cache = (cache_attn_k[l], cache_attn_v[l], cache_idx_k[l], state_mlstm_C[l],
                 state_mlstm_n[l], state_mlstm_m[l], state_conv[l])
        xs, ss = _layer(xs, ada_all[l, bp:bp + bs], p, cache, rows_per_batch=tsmp, alpha=alpha)
        st_p.append(sp)
        st_s.append(ss)

    def stack(sts, i):
        return jnp.stack([s[i] for s in sts], axis=0)

    return (xp.reshape(bp, tp, d), xs.reshape(bs, tsmp, d),
            *(stack(st_p, i) for i in range(7)), *(stack(st_s, i) for i in range(7)))
```

```python
import functools

import jax
import jax.numpy as jnp
from jax import lax
from jax.experimental import pallas as pl
from jax.experimental.pallas import tpu as pltpu

F32 = jnp.float32
BF16 = jnp.bfloat16

CHUNK = 64
A_HEADS = 8
A_KV_HEADS = 2
A_GROUP = A_HEADS // A_KV_HEADS
A_HEAD_DIM = 128
A_WIDTH = A_HEADS * A_HEAD_DIM
KV_WIDTH = A_KV_HEADS * A_HEAD_DIM
IDX_HEADS = 8
IDX_DIM = 64
TOPK_MAX = 256
M_HEADS = 4
M_DIM = 128
M_WIDTH = M_HEADS * M_DIM
CONV_CH = 512
CONV_W = 31
LN_EPS = 1e-5
NEG_INF = -1e30
ATTN_LOG2E_SCALE = A_HEAD_DIM ** -0.5 * 1.4426950408889634
INT_MIN = -2 ** 31

LANES = 128
VMEM_LIMIT_BYTES = 60 * 1024 * 1024
ADA_COLS = 1024
FFN_COLS = 1024

COL_Q = 0
COL_K = COL_Q + A_WIDTH
COL_V = COL_K + KV_WIDTH
COL_IQ = COL_V + KV_WIDTH
COL_MQ = COL_IQ + IDX_HEADS * IDX_DIM
COL_MK = COL_MQ + M_WIDTH
COL_MV = COL_MK + M_WIDTH
COL_MO = COL_MV + M_WIDTH
COL_CU = COL_MO + M_WIDTH
MAIN_COLS = COL_CU + 2 * CONV_CH
SM_IK = 0
SM_IW = SM_IK + IDX_DIM
SM_MI = SM_IW + IDX_HEADS
SM_MF = SM_MI + M_HEADS
SM_USED = SM_MF + M_HEADS

SLAB = 32 * 8

CONV_HALO = 32
CONV_PAD = CONV_HALO - (CONV_W - 1)
CONV_ROWS = 32


def _cparams(sem):
    return pltpu.CompilerParams(dimension_semantics=sem, vmem_limit_bytes=VMEM_LIMIT_BYTES)


def _normalize(x):
    mu = jnp.mean(x, axis=-1, keepdims=True)
    xc = x - mu
    var = jnp.mean(xc * xc, axis=-1, keepdims=True)
    return xc * lax.rsqrt(var + LN_EPS)


def _sigmoid(x):
    return 1.0 / (1.0 + jnp.exp(-x))


def _dot(a, b):
    return jnp.dot(a, b, preferred_element_type=F32)


def _dot_nt(a, b):
    return lax.dot_general(a, b, (((1,), (1,)), ((), ())), preferred_element_type=F32)


def _ada_kernel(c_ref, w_ref, b_ref, o_ref):
    c = c_ref[...]
    s = (c * _sigmoid(c)).astype(BF16)
    o_ref[0] = _dot(s, w_ref[0].astype(BF16)) + b_ref[0]


def _ada(c_all, w_ada, b_ada):
    depth, d, n = w_ada.shape
    rows = c_all.shape[0]
    tn = ADA_COLS
    return pl.pallas_call(
        _ada_kernel,
        grid=(depth, n // tn),
        in_specs=[pl.BlockSpec((rows, d), lambda l, j: (0, 0)),
                  pl.BlockSpec((1, d, tn), lambda l, j: (l, 0, j)),
                  pl.BlockSpec((1, 1, tn), lambda l, j: (l, 0, j))],
        out_specs=pl.BlockSpec((1, rows, tn), lambda l, j: (l, 0, j)),
        out_shape=jax.ShapeDtypeStruct((depth, rows, n), F32),
        compiler_params=_cparams(("arbitrary", "arbitrary")),
        name="ada",
    )(c_all, w_ada, b_ada.reshape(depth, 1, n))


def _mod_spec(per_row, tm, rows_per_batch, d):
    if per_row:
        return pl.BlockSpec((1, tm, d), lambda i, *_: (0, i, 0))
    bpb = rows_per_batch // tm
    return pl.BlockSpec((1, 1, d), lambda i, *_: (i // bpb, 0, 0))


def _inproj_kernel(x_ref, sh_ref, sc_ref, wm_ref, ws_ref, main_ref, k_ref, v_ref, small_ref, h_scr, *, kv_block):
    j = pl.program_id(1)

    @pl.when(j == 0)
    def _():
        h = _normalize(x_ref[...]) * (1.0 + sc_ref[0]) + sh_ref[0]
        hb = h.astype(BF16)
        h_scr[...] = hb
        small_ref[...] = _dot(hb, ws_ref[...])

    acc = _dot(h_scr[...], wm_ref[...])
    main_ref[...] = acc.astype(BF16)

    @pl.when(j == kv_block)
    def _():
        k_ref[...] = acc[:, :KV_WIDTH]
        v_ref[...] = acc[:, KV_WIDTH:]


def _inproj(x, sh, sc, w_main, w_small, layer, per_row, rows_per_batch, tm):
    m, d = x.shape
    tn = 2 * KV_WIDTH
    mod = _mod_spec(per_row, tm, rows_per_batch, d)
    return pl.pallas_call(
        functools.partial(_inproj_kernel, kv_block=COL_K // tn),
        grid=(m // tm, MAIN_COLS // tn),
        in_specs=[pl.BlockSpec((tm, d), lambda i, j: (i, 0)), mod, mod,
                  pl.BlockSpec((None, d, tn), lambda i, j: (layer, 0, j)),
                  pl.BlockSpec((None, d, LANES), lambda i, j: (layer, 0, 0))],
        out_specs=[pl.BlockSpec((tm, tn), lambda i, j: (i, j)),
                   pl.BlockSpec((tm, KV_WIDTH), lambda i, j: (i, 0)),
                   pl.BlockSpec((tm, KV_WIDTH), lambda i, j: (i, 0)),
                   pl.BlockSpec((tm, LANES), lambda i, j: (i, 0))],
        out_shape=[jax.ShapeDtypeStruct((m, MAIN_COLS), BF16),
                   jax.ShapeDtypeStruct((m, KV_WIDTH), F32),
                   jax.ShapeDtypeStruct((m, KV_WIDTH), F32),
                   jax.ShapeDtypeStruct((m, LANES), F32)],
        scratch_shapes=[pltpu.VMEM((tm, d), BF16)],
        compiler_params=_cparams(("arbitrary", "arbitrary")),
        name="inproj",
    )(x, sh, sc, w_main, w_small)


def _outproj_kernel(a_ref, m_ref, c_ref, x_ref, g_ref, w_ref, lg_ref, lb_ref, o_ref, *, alpha):
    y = _dot(a_ref[...], w_ref[0:A_WIDTH, :])
    y += _dot(m_ref[...], w_ref[A_WIDTH:A_WIDTH + M_WIDTH, :])
    y += _dot(c_ref[...], w_ref[A_WIDTH + M_WIDTH:, :])
    z = alpha * x_ref[...] + (1.0 + g_ref[0]) * y
    o_ref[...] = _normalize(z) * lg_ref[...] + lb_ref[...]


def _outproj(attn, mls, cnv, x, gate, w_out, layer, ln_g, ln_b, per_row, rows_per_batch, tm, alpha):
    m, d = x.shape
    row = lambda w: pl.BlockSpec((tm, w), lambda i: (i, 0))
    vec = pl.BlockSpec((1, d), lambda i: (0, 0))
    return pl.pallas_call(
        functools.partial(_outproj_kernel, alpha=alpha),
        grid=(m // tm,),
        in_specs=[row(A_WIDTH), row(M_WIDTH), row(CONV_CH), row(d),
                  _mod_spec(per_row, tm, rows_per_batch, d),
                  pl.BlockSpec((None,) + w_out.shape[1:], lambda i: (layer, 0, 0)), vec, vec],
        out_specs=row(d),
        out_shape=jax.ShapeDtypeStruct((m, d), F32),
        compiler_params=_cparams(("arbitrary",)),
        name="outproj",
    )(attn, mls, cnv, x, gate, w_out, ln_g.reshape(1, d), ln_b.reshape(1, d))


def _ffn_kernel(x_ref, sh_ref, sc_ref, g_ref, w1_ref, w2_ref, lg_ref, lb_ref, o_ref, h_scr, *, alpha):
    j = pl.program_id(1)

    @pl.when(j == 0)
    def _():
        h = _normalize(x_ref[...]) * (1.0 + sc_ref[0]) + sh_ref[0]
        h_scr[...] = h.astype(BF16)
        o_ref[...] = jnp.zeros_like(o_ref)

    u = jnp.maximum(_dot(h_scr[...], w1_ref[...]), 0.0)
    o_ref[...] += _dot((u * u).astype(BF16), w2_ref[...])

    @pl.when(j == pl.num_programs(1) - 1)
    def _():
        z = alpha * x_ref[...] + (1.0 + g_ref[0]) * o_ref[...]
        o_ref[...] = _normalize(z) * lg_ref[...] + lb_ref[...]


def _ffn(x, sh, sc, gate, w1, w2, layer, ln_g, ln_b, per_row, rows_per_batch, tm, alpha):
    m, d = x.shape
    dff = w1.shape[2]
    tf = FFN_COLS
    mod = _mod_spec(per_row, tm, rows_per_batch, d)
    vec = pl.BlockSpec((1, d), lambda i, j: (0, 0))
    return pl.pallas_call(
        functools.partial(_ffn_kernel, alpha=alpha),
        grid=(m // tm, dff // tf),
        in_specs=[pl.BlockSpec((tm, d), lambda i, j: (i, 0), pipeline_mode=pl.Buffered(1)), mod, mod, mod,
                  pl.BlockSpec((None, d, tf), lambda i, j: (layer, 0, j)),
                  pl.BlockSpec((None, tf, d), lambda i, j: (layer, j, 0)), vec, vec],
        out_specs=pl.BlockSpec((tm, d), lambda i, j: (i, 0)),
        out_shape=jax.ShapeDtypeStruct((m, d), F32),
        scratch_shapes=[pltpu.VMEM((tm, d), BF16)],
        compiler_params=_cparams(("arbitrary", "arbitrary")),
        name="ffn",
    )(x, sh, sc, gate, w1, w2, ln_g.reshape(1, d), ln_b.reshape(1, d))


def _dsa_kernel(q_ref, qi_ref, wi_ref, k_ref, v_ref, ki_ref, o_ref,
                key_scr, plane_scr, alive_scr, qs_scr, qis_scr, s_scr, m_scr, l_scr, acc_scr,
                *, tq, ts, q_off, n_keys, topk, n_tiles_static):
    qb = pl.program_id(1)
    q_pos0 = q_off + qb * tq
    if n_tiles_static is None:
        vis_max = jnp.minimum(((q_pos0 + tq - 1) // CHUNK + 1) * CHUNK, n_keys)
        n_tiles = (vis_max + ts - 1) // ts
    else:
        n_tiles = n_tiles_static
    lane_tiles = [slice(c * LANES, (c + 1) * LANES) for c in range(ts // LANES)]
    n_slabs = ts // SLAB

    q_chunk = (q_pos0 + lax.broadcasted_iota(jnp.int32, (1, tq), 1)) // CHUNK
    n_vis = jnp.minimum((q_chunk + 1) * CHUNK, n_keys)
    kk = jnp.minimum(topk, n_vis).astype(F32)

    eye = (lax.broadcasted_iota(jnp.int32, (tq, tq), 0)
           == lax.broadcasted_iota(jnp.int32, (tq, tq), 1)).astype(BF16)
    for h in range(A_HEADS):
        rows = slice((h % A_GROUP) * tq, (h % A_GROUP + 1) * tq)
        qs_scr[h // A_GROUP, rows, 0:A_HEAD_DIM] = q_ref[0, :, h * A_HEAD_DIM:(h + 1) * A_HEAD_DIM]
        qs_scr[h // A_GROUP, rows, A_HEAD_DIM:A_HEAD_DIM + tq] = eye
    for h in range(IDX_HEADS):
        qis_scr[h // 2, (h % 2) * tq:(h % 2 + 1) * tq, :] = qi_ref[0, :, h * IDX_DIM:(h + 1) * IDX_DIM]
    wi = wi_ref[0] * IDX_HEADS ** -0.5 * IDX_DIM ** -0.5

    def score_tile(t):
        start = pl.multiple_of(t * ts, ts)
        kt = ki_ref[0, pl.ds(start, ts), :]
        score = None
        for pair in range(IDX_HEADS // 2):
            d = _dot_nt(kt, qis_scr[pair])
            for e in range(2):
                h = 2 * pair + e
                term = wi[h:h + 1, :] * jnp.maximum(d[:, e * tq:(e + 1) * tq], 0.0)
                score = term if score is None else score + term
        score = jnp.where(score == 0.0, 0.0, score)
        bits = lax.bitcast_convert_type(score, jnp.int32)
        key = bits ^ ((bits >> 31) & 0x7FFFFFFF)
        visible = lax.broadcasted_iota(jnp.int32, (ts, tq), 0) < n_vis - start
        key = jnp.where(visible, key, INT_MIN)
        key_scr[t] = key
        for sl in range(n_slabs):
            a = [key[sl * SLAB + 8 * j:sl * SLAB + 8 * j + 8, :] ^ INT_MIN for j in range(32)]
            dist, mask = 16, 0x0000FFFF
            while dist:
                k = 0
                while k < 32:
                    swap = (a[k] ^ lax.shift_right_logical(a[k + dist], dist)) & mask
                    a[k] = a[k] ^ swap
                    a[k + dist] = a[k + dist] ^ (swap << dist)
                    k = (k + dist + 1) & ~dist
                dist >>= 1
                mask ^= (mask << dist) & 0xFFFFFFFF
            present = a[0]
            for i in range(32):
                plane_scr[t, sl, i] = a[i]
                present = present | a[i]
            alive_scr[t, sl] = present

    def score_pair(i, carry):
        score_tile(2 * i)
        score_tile(2 * i + 1)
        return carry

    lax.fori_loop(0, n_tiles // 2, score_pair, 0)

    @pl.when(n_tiles % 2 == 1)
    def _():
        score_tile(n_tiles - 1)

    @pl.when(n_tiles % 2 == 1)
    def _():
        plane_scr[n_tiles] = jnp.zeros(plane_scr.shape[1:], jnp.int32)
        alive_scr[n_tiles] = jnp.zeros(alive_scr.shape[1:], jnp.int32)

    def sweep(i, take_prev):
        def body(pair, accs):
            accs = list(accs)
            for e in range(2):
                t = 2 * pair + e
                for sl in range(n_slabs):
                    a = alive_scr[t, sl]
                    if take_prev is not None:
                        x = a & plane_scr[t, sl, i - 1]
                        a = jnp.where(take_prev, x, a ^ x)
                        alive_scr[t, sl] = a
                    accs[e] = accs[e] + lax.population_count(a & plane_scr[t, sl, i])
            return tuple(accs)
        zero = jnp.zeros((8, tq), jnp.int32)
        acc0, acc1 = lax.fori_loop(0, (n_tiles + 1) // 2, body, (zero, zero))
        return jnp.sum((acc0 + acc1).astype(F32), axis=0, keepdims=True)

    def decide(i, cnt, want, n_alive, t_u):
        take = cnt >= want
        want = jnp.where(take, want, want - cnt)
        n_alive = jnp.where(take, cnt, n_alive - cnt)
        t_u = t_u | jnp.where(take, lax.shift_left(jnp.int32(1), 31 - i), 0)
        return take.astype(jnp.int32), want, n_alive, t_u

    def radix_step(i, carry):
        take_prev, want, n_alive, t_u = carry
        cnt = sweep(i, jnp.broadcast_to(take_prev, (8, tq)) != 0)
        return decide(i, cnt, want, n_alive, t_u)

    carry = decide(0, sweep(0, None), kk, n_vis.astype(F32), jnp.zeros((1, tq), jnp.int32))
    _, keep, n_alive, t_u = lax.fori_loop(1, 32, radix_step, carry)
    thr = jnp.broadcast_to(t_u ^ INT_MIN, (8, tq))

    @pl.when(jnp.max(n_alive - keep) > 0.0)
    def _():
        lower = (lax.broadcasted_iota(jnp.int32, (ts, ts), 0)
                 >= lax.broadcasted_iota(jnp.int32, (ts, ts), 1)).astype(BF16)

        def tie_tile(t, seen):
            blk = key_scr[t]
            eq = blk == thr[0:1, :]
            rank = seen + _dot(lower, jnp.where(eq, 1.0, 0.0).astype(BF16))
            key_scr[t] = jnp.where(eq & (rank > keep), INT_MIN, blk)
            return rank[ts - 1:ts, :]

        lax.fori_loop(0, n_tiles, tie_tile, jnp.zeros((1, tq), F32))

    m_scr[...] = jnp.full(m_scr.shape, NEG_INF, F32)
    l_scr[...] = jnp.zeros(l_scr.shape, F32)
    acc_scr[...] = jnp.zeros(acc_scr.shape, F32)

    def logits_tile(t, slot):
        start = pl.multiple_of(t * ts, ts)
        bias = jnp.where(key_scr[t] >= thr[0:1, :], 0.0, NEG_INF).astype(BF16)
        for n in range(A_KV_HEADS):
            kt = k_ref[0, pl.ds(start, ts), n * A_HEAD_DIM:(n + 1) * A_HEAD_DIM]
            s_scr[slot, n] = _dot_nt(qs_scr[n], jnp.concatenate([kt, bias], axis=1))

    def softmax_pv_tile(t, slot):
        start = pl.multiple_of(t * ts, ts)
        for n in range(A_KV_HEADS):
            vt = v_ref[0, pl.ds(start, ts), n * A_HEAD_DIM:(n + 1) * A_HEAD_DIM]
            for g in range(A_GROUP):
                h = n * A_GROUP + g
                s = [s_scr[slot, n, g * tq:(g + 1) * tq, ls] for ls in lane_tiles]
                mx = s[0]
                for sc in s[1:]:
                    mx = jnp.maximum(mx, sc)
                m_old = m_scr[h]
                m_new = jnp.maximum(m_old, jnp.max(mx, axis=1, keepdims=True))
                alpha = jnp.exp2(m_old - m_new)
                p = [jnp.exp2(sc - m_new) for sc in s]
                psum = p[0]
                for pc in p[1:]:
                    psum = psum + pc
                l_scr[h] = alpha * l_scr[h] + psum
                pv = _dot(jnp.concatenate(p, axis=1).astype(BF16), vt)
                acc_scr[h] = alpha * acc_scr[h] + pv
                m_scr[h] = m_new

    def attend_pair(i, carry):
        t = 2 * i
        logits_tile(t + 1, 1)
        softmax_pv_tile(t, 0)
        logits_tile(t + 2, 0)
        softmax_pv_tile(t + 1, 1)
        return carry

    n_pairs = (n_tiles - 1) // 2
    logits_tile(0, 0)
    lax.fori_loop(0, n_pairs, attend_pair, 0)
    t_rest = 2 * n_pairs

    @pl.when(t_rest + 1 < n_tiles)
    def _():
        logits_tile(t_rest + 1, 1)
        softmax_pv_tile(t_rest, 0)
        softmax_pv_tile(t_rest + 1, 1)

    @pl.when(t_rest + 1 >= n_tiles)
    def _():
        softmax_pv_tile(t_rest, 0)

    for h in range(A_HEADS):
        l_tot = jnp.sum(l_scr[h], axis=1, keepdims=True)
        o_ref[0, :, h * A_HEAD_DIM:(h + 1) * A_HEAD_DIM] = (acc_scr[h] / l_tot).astype(BF16)


def _dsa(main3, keys_src, ki, wi, *, k_col, v_col, q_off, n_keys, tq, ts):
    assert tq == A_HEAD_DIM, "the mask rides in the unused half of a 2*A_HEAD_DIM-deep contraction"
    b, t_q, _ = main3.shape
    s_len = keys_src.shape[1]
    topk = min(TOPK_MAX, n_keys // 4)
    dynamic = q_off == 0
    kern = functools.partial(_dsa_kernel, tq=tq, ts=ts, q_off=q_off, n_keys=n_keys, topk=topk,
                             n_tiles_static=None if dynamic else s_len // ts)
    return pl.pallas_call(
        kern,
        grid=(b, t_q // tq),
        in_specs=[pl.BlockSpec((1, tq, A_WIDTH), lambda i, j: (i, j, COL_Q // A_WIDTH)),
                  pl.BlockSpec((1, tq, IDX_HEADS * IDX_DIM), lambda i, j: (i, j, COL_IQ // (IDX_HEADS * IDX_DIM))),
                  pl.BlockSpec((1, IDX_HEADS, tq), lambda i, j: (i, 0, j)),
                  pl.BlockSpec((1, s_len, KV_WIDTH), lambda i, j: (i, 0, k_col)),
                  pl.BlockSpec((1, s_len, KV_WIDTH), lambda i, j: (i, 0, v_col)),
                  pl.BlockSpec((1, s_len, IDX_DIM), lambda i, j: (i, 0, 0))],
        out_specs=pl.BlockSpec((1, tq, A_WIDTH), lambda i, j: (i, j, 0)),
        out_shape=jax.ShapeDtypeStruct((b, t_q, A_WIDTH), BF16),
        scratch_shapes=[pltpu.VMEM((s_len // ts, ts, tq), jnp.int32),
                        pltpu.VMEM((s_len // ts + 1, ts // SLAB, 32, 8, tq), jnp.int32),
                        pltpu.VMEM((s_len // ts + 1, ts // SLAB, 8, tq), jnp.int32),
                        pltpu.VMEM((A_KV_HEADS, A_GROUP * tq, A_HEAD_DIM + tq), BF16),
                        pltpu.VMEM((IDX_HEADS // 2, 2 * tq, IDX_DIM), BF16),
                        pltpu.VMEM((2, A_KV_HEADS, A_GROUP * tq, ts), F32),
                        pltpu.VMEM((A_HEADS, tq, LANES), F32),
                        pltpu.VMEM((A_HEADS, tq, LANES), F32),
                        pltpu.VMEM((A_HEADS, tq, A_HEAD_DIM), F32)],
        compiler_params=_cparams(("arbitrary", "arbitrary")),
        name="dsa",
    )(main3, main3, wi, keys_src, keys_src, ki)


def _log_sigmoid(x):
    return jnp.minimum(x, 0.0) - jnp.log1p(jnp.exp(-jnp.abs(x)))


def _mlstm_kernel(q_ref, k_ref, v_ref, o_ref, g_ref, gb_ref, ng_ref, c0_ref, n0_ref, m0_ref,
                  y_ref, c_out, n_out, m_out, c_scr, n_scr, m_scr, *, chunk, valid_len, batches):
    step = pl.program_id(1)

    @pl.when(step == 0)
    def _():
        c_scr[...] = c0_ref[...]
        n_scr[...] = n0_ref[...]
        m_scr[...] = m0_ref[...]

    for bi in range(batches):
        _mlstm_chunk(bi, q_ref, k_ref, v_ref, o_ref, g_ref, gb_ref, ng_ref, y_ref, c_scr, n_scr, m_scr,
                     chunk=chunk, valid_len=valid_len)

    c_out[...] = c_scr[...]
    n_out[...] = n_scr[...]
    m_out[...] = m_scr[...]


def _mlstm_chunk(bi, q_ref, k_ref, v_ref, o_ref, g_ref, gb_ref, ng_ref, y_ref, c_scr, n_scr, m_scr,
                 *, chunk, valid_len):
    L = chunk
    g = g_ref[bi] + gb_ref[...]
    lf = _log_sigmoid(g)
    if valid_len < L:
        live = lax.broadcasted_iota(jnp.int32, (L, 1), 0) < valid_len
        lf = jnp.where(live, lf, 0.0)
        g = jnp.where(live, g, NEG_INF)
    g_t = g.T
    lf_t = lf.T
    row = lax.broadcasted_iota(jnp.int32, (L, L), 0)
    col = lax.broadcasted_iota(jnp.int32, (L, L), 1)
    lower = row >= col
    b_c = jnp.dot(jnp.where(lower, 1.0, 0.0), lf, precision=lax.Precision.HIGHEST,
                  preferred_element_type=F32)
    b_r = jnp.dot(lf_t, jnp.where(lower, 0.0, 1.0) + jnp.where(row == col, 1.0, 0.0),
                  precision=lax.Precision.HIGHEST, preferred_element_type=F32)

    for h in range(M_HEADS):
        sl = slice(h * M_DIM, (h + 1) * M_DIM)
        b_col = b_c[:, SM_MF + h:SM_MF + h + 1]
        ig_col = g[:, SM_MI + h:SM_MI + h + 1]
        a_row = g_t[SM_MI + h:SM_MI + h + 1, :] - b_r[SM_MF + h:SM_MF + h + 1, :]
        m_prev = m_scr[bi, h, 0:1, 0:1]
        d_mat = jnp.where(lower, b_col + a_row, NEG_INF)
        inter = b_col + m_prev
        m_t = jnp.maximum(inter, jnp.max(d_mat, axis=1, keepdims=True))
        w_intra = jnp.exp(d_mat - m_t)
        w_inter = jnp.exp(inter - m_t)
        qh = q_ref[bi, :, sl]
        kf = k_ref[bi, :, sl].astype(F32) * M_DIM ** -0.5
        vh = v_ref[bi, :, sl]
        s = _dot_nt(qh, kf.astype(BF16)) * w_intra
        c_old = c_scr[bi, h]
        n_old = n_scr[bi, h, 0:1, :]
        num = w_inter * _dot(qh, c_old.astype(BF16)) + _dot(s.astype(BF16), vh)
        den = (w_inter * jnp.sum(qh.astype(F32) * n_old, axis=1, keepdims=True)
               + jnp.sum(s, axis=1, keepdims=True))
        hid = num / jnp.maximum(jnp.abs(den), jnp.exp(-m_t))
        m_end = m_t[L - 1:L, :]
        b_last = b_col[L - 1:L, :]
        w_end = jnp.exp(b_last - b_col + ig_col - m_end)
        decay = jnp.exp(b_last + m_prev - m_end)
        kw = w_end * kf
        c_scr[bi, h] = decay * c_old + _dot(kw.T.astype(BF16), vh)
        n_scr[bi, h] = jnp.broadcast_to(decay * n_old + jnp.sum(kw, axis=0, keepdims=True), (8, M_DIM))
        m_scr[bi, h] = jnp.broadcast_to(m_end, (8, LANES))
        gated = _sigmoid(o_ref[bi, :, sl].astype(F32)) * hid
        y_ref[bi, :, sl] = (_normalize(gated) * ng_ref[:, sl]).astype(BF16)


def _mlstm(main3, small3, gate_bias, norm_g, c0, n0, m0, *, chunk, valid_len, batches):
    b, t, _ = main3.shape
    bb = batches
    blk = lambda col: pl.BlockSpec((bb, chunk, M_WIDTH), lambda i, j: (i, j, col // M_WIDTH))
    st = lambda shape: pl.BlockSpec((bb,) + shape, lambda i, j: (i,) + (0,) * len(shape))
    c_shape, n_shape, m_shape = (M_HEADS, M_DIM, M_DIM), (M_HEADS, 8, M_DIM), (M_HEADS, 8, LANES)
    return pl.pallas_call(
        functools.partial(_mlstm_kernel, chunk=chunk, valid_len=valid_len, batches=bb),
        grid=(b // bb, t // chunk),
        in_specs=[blk(COL_MQ), blk(COL_MK), blk(COL_MV), blk(COL_MO),
                  pl.BlockSpec((bb, chunk, LANES), lambda i, j: (i, j, 0)),
                  pl.BlockSpec((1, LANES), lambda i, j: (0, 0)),
                  pl.BlockSpec((1, M_WIDTH), lambda i, j: (0, 0)),
                  st(c_shape), st(n_shape), st(m_shape)],
        out_specs=[pl.BlockSpec((bb, chunk, M_WIDTH), lambda i, j: (i, j, 0)),
                   st(c_shape), st(n_shape), st(m_shape)],
        out_shape=[jax.ShapeDtypeStruct((b, t, M_WIDTH), BF16),
                   jax.ShapeDtypeStruct((b,) + c_shape, F32),
                   jax.ShapeDtypeStruct((b,) + n_shape, F32),
                   jax.ShapeDtypeStruct((b,) + m_shape, F32)],
        scratch_shapes=[pltpu.VMEM((bb,) + c_shape, F32), pltpu.VMEM((bb,) + n_shape, F32),
                        pltpu.VMEM((bb,) + m_shape, F32)],
        compiler_params=_cparams(("arbitrary", "arbitrary")),
        name="mlstm",
    )(main3, main3, main3, main3, small3, gate_bias, norm_g.reshape(1, M_WIDTH), c0, n0, m0)


def _conv_kernel(cu_ref, halo_ref, st_ref, w_ref, cb_ref, ng_ref, nb_ref, y_ref, ns_ref, full_scr, sh_scr,
                 *, tm, valid_len):
    i = pl.program_id(1)

    def glu(u):
        return u[:, :CONV_CH].astype(F32) * _sigmoid(u[:, CONV_CH:].astype(F32))

    @pl.when(i == 0)
    def _():
        full_scr[0:CONV_HALO, :] = st_ref[0]

    @pl.when(i > 0)
    def _():
        full_scr[0:CONV_HALO, :] = glu(halo_ref[0])

    full_scr[CONV_HALO:CONV_HALO + tm, :] = glu(cu_ref[0])
    span = tm + CONV_HALO - 8
    for b in range(1, 8):
        sh_scr[b - 1] = full_scr[pl.ds(b, span), :]
    for c0 in range(0, tm, CONV_ROWS):
        acc = None
        for j in range(CONV_W):
            a8, b = (CONV_PAD + j) // 8 * 8, (CONV_PAD + j) % 8
            if b == 0:
                src = full_scr[pl.ds(c0 + a8, CONV_ROWS), :]
            else:
                src = sh_scr[b - 1, pl.ds(c0 + a8, CONV_ROWS), :]
            term = w_ref[j:j + 1, :] * src
            acc = term if acc is None else acc + term
        y = _normalize(acc + cb_ref[...]) * ng_ref[...] + nb_ref[...]
        y_ref[0, c0:c0 + CONV_ROWS, :] = (y * _sigmoid(y)).astype(BF16)
    ns_ref[0] = full_scr[pl.ds(valid_len, CONV_HALO), :]


def _conv(main3, state, w, bias, ng, nb, *, tm, valid_len):
    b, t, _ = main3.shape
    cu_blk = COL_CU // (2 * CONV_CH)
    hpb = tm // CONV_HALO
    vec = pl.BlockSpec((1, CONV_CH), lambda i, j: (0, 0))
    return pl.pallas_call(
        functools.partial(_conv_kernel, tm=tm, valid_len=valid_len),
        grid=(b, t // tm),
        in_specs=[pl.BlockSpec((1, tm, 2 * CONV_CH), lambda i, j: (i, j, cu_blk)),
                  pl.BlockSpec((1, CONV_HALO, 2 * CONV_CH), lambda i, j: (i, jnp.maximum(j * hpb - 1, 0), cu_blk)),
                  pl.BlockSpec((1, CONV_HALO, CONV_CH), lambda i, j: (i, 0, 0)),
                  pl.BlockSpec((CONV_W, CONV_CH), lambda i, j: (0, 0)), vec, vec, vec],
        out_specs=[pl.BlockSpec((1, tm, CONV_CH), lambda i, j: (i, j, 0)),
                   pl.BlockSpec((1, CONV_HALO, CONV_CH), lambda i, j: (i, 0, 0))],
        out_shape=[jax.ShapeDtypeStruct((b, t, CONV_CH), BF16),
                   jax.ShapeDtypeStruct((b, CONV_HALO, CONV_CH), F32)],
        scratch_shapes=[pltpu.VMEM((CONV_HALO + tm, CONV_CH), F32),
                        pltpu.VMEM((7, CONV_HALO + tm - 8, CONV_CH), F32)],
        compiler_params=_cparams(("arbitrary", "arbitrary")),
        name="conv",
    )(main3, main3, state, w, bias.reshape(1, CONV_CH), ng.reshape(1, CONV_CH), nb.reshape(1, CONV_CH))


def _split_w_in(w_in):
    sizes = (A_WIDTH, KV_WIDTH, KV_WIDTH, IDX_HEADS * IDX_DIM, IDX_DIM, IDX_HEADS,
             M_WIDTH, M_WIDTH, M_WIDTH, M_HEADS, M_HEADS, M_WIDTH, 2 * CONV_CH)
    offs = [0]
    for size in sizes:
        offs.append(offs[-1] + size)
    aq, ak, av, iq, ik, iw, mq, mk, mv, mi, mf, mo, cu, end = offs
    w_main = jnp.concatenate([w_in[..., aq:ik], w_in[..., mq:mi], w_in[..., mo:end]], axis=-1)
    col_scale = jnp.where(jnp.arange(MAIN_COLS) < A_WIDTH, ATTN_LOG2E_SCALE, 1.0).astype(w_in.dtype)
    w_main = (w_main * col_scale).astype(BF16)
    pad = jnp.zeros(w_in.shape[:-1] + (LANES - SM_USED,), w_in.dtype)
    w_small = jnp.concatenate([w_in[..., ik:mq], w_in[..., mi:mo], pad], axis=-1).astype(BF16)
    return w_main, w_small


def _tiles(m, rows_per_batch, per_row):
    if per_row:
        return dict(inproj=m, outproj=m, ffn=m, dsa_q=128, dsa_k=512, mlstm=128, mlstm_batches=1, conv=128)
    t = rows_per_batch
    return dict(inproj=min(1024, t), outproj=min(512, t), ffn=min(1024, t),
                dsa_q=min(128, t), dsa_k=min(512, t), mlstm=min(256, t), mlstm_batches=1, conv=min(512, t))


def _layer(x, ada, p, cache, *, rows_per_batch, alpha):
    m, d = x.shape
    nb = m // rows_per_batch
    t = rows_per_batch
    per_row = cache is not None
    tiles = _tiles(m, t, per_row)
    if per_row:
        mods = [jnp.repeat(v, t, axis=0)[None] for v in jnp.split(ada, 6, axis=-1)]
    else:
        mods = [v[:, None, :] for v in jnp.split(ada, 6, axis=-1)]
    sh1, sc1, g1, sh2, sc2, g2 = mods

    layer = p['layer']
    main, k_rows, v_rows, small = _inproj(x, sh1, sc1, p['w_main'], p['w_small'], layer, per_row, t,
                                          tiles['inproj'])
    k_new = k_rows.reshape(nb, t, A_KV_HEADS, A_HEAD_DIM)
    v_new = v_rows.reshape(nb, t, A_KV_HEADS, A_HEAD_DIM)
    ik_new = small[:, SM_IK:SM_IK + IDX_DIM].reshape(nb, t, IDX_DIM)

    gate_bias = jnp.zeros((1, LANES), F32)
    gate_bias = gate_bias.at[0, SM_MI:SM_MI + M_HEADS].set(p['b_igate'])
    gate_bias = gate_bias.at[0, SM_MF:SM_MF + M_HEADS].set(p['b_fgate'])

    if cache is None:
        main3 = main.reshape(nb, t, MAIN_COLS)
        small3 = small.reshape(nb, t, LANES)
        ki = ik_new.astype(BF16)
        wi = jnp.swapaxes(small3[:, :, SM_IW:SM_IW + IDX_HEADS], 1, 2)
        attn = _dsa(main3, main3, ki, wi, k_col=COL_K // KV_WIDTH, v_col=COL_V // KV_WIDTH,
                    q_off=0, n_keys=t, tq=tiles['dsa_q'], ts=tiles['dsa_k'])
        c0 = jnp.zeros((nb, M_HEADS, M_DIM, M_DIM), F32)
        n0 = jnp.zeros((nb, M_HEADS, 8, M_DIM), F32)
        m0 = jnp.zeros((nb, M_HEADS, 8, LANES), F32)
        mls, c_new, n_new, m_new = _mlstm(main3, small3, gate_bias, p['mlstm_norm_g'], c0, n0, m0,
                                          chunk=tiles['mlstm'], valid_len=tiles['mlstm'],
                                          batches=tiles['mlstm_batches'])
        state = jnp.zeros((nb, CONV_HALO, CONV_CH), F32)
        cnv, conv_new = _conv(main3, state, p['conv_w'], p['conv_b'], p['conv_norm_g'], p['conv_norm_b'],
                              tm=tiles['conv'], valid_len=tiles['conv'])
        attn, mls, cnv = (a.reshape(m, -1) for a in (attn, mls, cnv))
    else:
        ck, cv, cik, c_st, n_st, m_st, conv_st = cache
        past = ck.shape[1]
        n_keys = past + t
        tp = tiles['dsa_q']
        ts = tiles['dsa_k']
        s_pad = -(-n_keys // ts) * ts
        main3 = jnp.pad(main.reshape(nb, t, MAIN_COLS), ((0, 0), (0, tp - t), (0, 0)))
        small3 = jnp.pad(small.reshape(nb, t, LANES), ((0, 0), (0, tp - t), (0, 0)))
        pad_keys = lambda a: jnp.pad(a, ((0, 0), (0, s_pad - n_keys), (0, 0)))
        k_all = pad_keys(jnp.concatenate([ck.reshape(nb, past, KV_WIDTH), k_rows.reshape(nb, t, KV_WIDTH)], axis=1))
        v_all = pad_keys(jnp.concatenate([cv.reshape(nb, past, KV_WIDTH), v_rows.reshape(nb, t, KV_WIDTH)], axis=1))
        kv_all = jnp.concatenate([k_all, v_all], axis=-1).astype(BF16)
        ki = pad_keys(jnp.concatenate([cik, ik_new], axis=1)).astype(BF16)
        wi = jnp.swapaxes(small3[:, :, SM_IW:SM_IW + IDX_HEADS], 1, 2)
        attn = _dsa(main3, kv_all, ki, wi, k_col=0, v_col=1, q_off=past, n_keys=n_keys, tq=tp, ts=ts)
        n0 = jnp.broadcast_to(n_st[:, :, None, :], (nb, M_HEADS, 8, M_DIM))
        m0 = jnp.broadcast_to(m_st[:, :, None, None], (nb, M_HEADS, 8, LANES))
        mls, c_new, n_new, m_new = _mlstm(main3, small3, gate_bias, p['mlstm_norm_g'], c_st, n0, m0,
                                          chunk=tp, valid_len=t, batches=tiles['mlstm_batches'])
        state = jnp.pad(conv_st, ((0, 0), (CONV_PAD, 0), (0, 0)))
        cnv, conv_new = _conv(main3, state, p['conv_w'], p['conv_b'], p['conv_norm_g'], p['conv_norm_b'],
                              tm=tp, valid_len=t)
        attn, mls, cnv = (a[:, :t].reshape(m, -1) for a in (attn, mls, cnv))

    x = _outproj(attn, mls, cnv, x, g1, p['w_out'], layer, p['ln1_g'], p['ln1_b'], per_row, t,
                 tiles['outproj'], alpha)
    x = _ffn(x, sh2, sc2, g2, p['w_ff1'], p['w_ff2'], layer, p['ln2_g'], p['ln2_b'], per_row, t,
             tiles['ffn'], alpha)
    new_state = (k_new, v_new, ik_new, c_new, n_new[:, :, 0, :], m_new[:, :, 0, 0], conv_new[:, CONV_PAD:, :])
    return x, new_state


def kernel(x_prompt, x_sample, c_prompt, c_sample, cache_attn_k, cache_attn_v, cache_idx_k,
           state_mlstm_C, state_mlstm_n, state_mlstm_m, state_conv,
           w_ada, b_ada, w_in, b_igate, b_fgate, mlstm_norm_g, conv_w, conv_b,
           conv_norm_g, conv_norm_b, w_out, ln1_g, ln1_b, w_ff1, w_ff2, ln2_g, ln2_b):
    depth = w_ada.shape[0]
    alpha = (2 * depth) ** 0.25
    bp, tp, d = x_prompt.shape
    bs, tsmp, _ = x_sample.shape

    c_all = jnp.concatenate([c_prompt, c_sample], axis=0)
    rows = -(-c_all.shape[0] // 8) * 8
    ada_all = _ada(jnp.pad(c_all, ((0, rows - c_all.shape[0]), (0, 0))), w_ada, b_ada)

    xp = x_prompt.reshape(bp * tp, d)
    xs = x_sample.reshape(bs * tsmp, d)
    st_p, st_s = [], []
    w_main, w_small = _split_w_in(w_in)
    w_out_b, w_ff1_b, w_ff2_b = w_out.astype(BF16), w_ff1.astype(BF16), w_ff2.astype(BF16)
    for l in range(depth):
        p = {'layer': l, 'w_main': w_main, 'w_small': w_small, 'b_igate': b_igate[l], 'b_fgate': b_fgate[l],
             'mlstm_norm_g': mlstm_norm_g[l], 'conv_w': conv_w[l], 'conv_b': conv_b[l],
             'conv_norm_g': conv_norm_g[l], 'conv_norm_b': conv_norm_b[l],
             'w_out': w_out_b, 'ln1_g': ln1_g[l], 'ln1_b': ln1_b[l],
             'w_ff1': w_ff1_b, 'w_ff2': w_ff2_b,
             'ln2_g': ln2_g[l], 'ln2_b': ln2_b[l]}
        xp, sp = _layer(xp, ada_all[l, :bp], p, None, rows_per_batch=tp, alpha=alpha)
        cache = (cache_attn_k[l], cache_attn_v[l], cache_idx_k[l], state_mlstm_C[l],
                 state_mlstm_n[l], state_mlstm_m[l], state_conv[l])
        xs, ss = _layer(xs, ada_all[l, bp:bp + bs], p, cache, rows_per_batch=tsmp, alpha=alpha)
        st_p.append(sp)
        st_s.append(ss)

    def stack(sts, i):
        return jnp.stack([s[i] for s in sts], axis=0)

    return (xp.reshape(bp, tp, d), xs.reshape(bs, tsmp, d),
            *(stack(st_p, i) for i in range(7)), *(stack(st_s, i) for i in range(7)))
```
